```python
import math
import jax, jax.numpy as jnp
from jax import lax
import numpy as np

D_MODEL = 2048
BATCH = 2
SEQ = 4096
DEPTH = 1
DEC_BATCH = 4
DEC_SEQ = 8192
PAST_LEN = 128

HEAD_DIM = 128
N_HEADS_TOTAL = D_MODEL // HEAD_DIM
A_HEADS = N_HEADS_TOTAL // 2
A_KV_HEADS = 2
A_GROUP = A_HEADS // A_KV_HEADS
WINDOW = 128
BLOCK = 128
B_HEADS = N_HEADS_TOTAL - A_HEADS
B_QK_DIM = HEAD_DIM // 2
B_V_DIM = HEAD_DIM
MIX_WIDTH = A_HEADS * HEAD_DIM + B_HEADS * B_V_DIM
D_FF = 4 * D_MODEL
EPS = 1e-5
A_Q_W = A_HEADS * HEAD_DIM
A_KV_W = A_KV_HEADS * HEAD_DIM
B_QK_W = B_HEADS * 2 * B_QK_DIM
B_V_W = B_HEADS * B_V_DIM
IN_WIDTH = A_Q_W + 2 * A_KV_W + 2 * B_QK_W + B_V_W
SPLITS = [A_Q_W, A_Q_W + A_KV_W, A_Q_W + 2 * A_KV_W,
          A_Q_W + 2 * A_KV_W + B_QK_W, A_Q_W + 2 * A_KV_W + 2 * B_QK_W]

kernel_name = "hymba_window_sink_diffattn_alibi_encoder"


def _rmsnorm(x, g):
    x32 = x.astype(jnp.float32)
    y = x32 * lax.rsqrt(jnp.mean(x32 * x32, axis=-1, keepdims=True) + EPS)
    return (y * g.astype(jnp.float32)).astype(x.dtype)


def _alibi_slopes():
    i = jnp.arange(1, N_HEADS_TOTAL + 1, dtype=jnp.float32)
    s = jnp.exp2(-8.0 / N_HEADS_TOTAL * i)
    return s[0::2], s[1::2]


def _band(t):
    b, s, h, d = t.shape
    tb = t.reshape(b, s // BLOCK, BLOCK, h, d)
    tp = jnp.pad(tb, ((0, 0), (1, 1), (0, 0), (0, 0), (0, 0)))
    return jnp.concatenate([tp[:, :-2], tp[:, 1:-1], tp[:, 2:]], axis=2)


def _window_gqa_sink(q, k, v, sink, slopes):
    b, s = q.shape[0], q.shape[1]
    nb = s // BLOCK
    qb = q.reshape(b, nb, BLOCK, A_KV_HEADS, A_GROUP, HEAD_DIM)
    kb, vb = _band(k), _band(v)
    scale = 1.0 / math.sqrt(HEAD_DIM)
    sc = jnp.einsum('bnqhgd,bnkhd->bnhgqk', qb, kb).astype(jnp.float32) * scale
    qpos = jnp.arange(s, dtype=jnp.int32).reshape(nb, BLOCK)
    kpos = (jnp.arange(nb, dtype=jnp.int32)[:, None] - 1) * BLOCK + jnp.arange(3 * BLOCK, dtype=jnp.int32)[None]
    dist = jnp.abs(qpos[:, :, None] - kpos[:, None, :])
    valid = (dist <= WINDOW) & (kpos >= 0)[:, None, :] & (kpos < s)[:, None, :]
    bias = -slopes.reshape(A_KV_HEADS, A_GROUP)[None, :, :, None, None] * dist.astype(jnp.float32)[:, None, None]
    sc = jnp.where(valid[:, None, None], sc + bias, -jnp.inf)
    snk = sink.astype(jnp.float32).reshape(A_KV_HEADS, A_GROUP, 1, 1)
    m = jnp.maximum(jnp.max(sc, axis=-1, keepdims=True), snk)
    p = jnp.exp(sc - m)
    p = p / (jnp.sum(p, axis=-1, keepdims=True) + jnp.exp(snk - m))
    o = jnp.einsum('bnhgqk,bnkhd->bnqhgd', p.astype(v.dtype), vb)
    return o.reshape(b, s, A_HEADS * HEAD_DIM)


def _diff_attention(q, k, v, lam, slopes, subln_g, lambda_init):
    b, s = q.shape[0], q.shape[1]
    nb = s // BLOCK
    qb = q.reshape(b, nb, BLOCK, B_HEADS, 2, B_QK_DIM).transpose(1, 0, 3, 4, 2, 5)
    kt = k.transpose(0, 2, 3, 1, 4)
    vt = v.transpose(0, 2, 1, 3)
    kpos = jnp.arange(s, dtype=jnp.int32)
    scale = 1.0 / math.sqrt(B_QK_DIM)
    sl = slopes[:, None, None, None]

    def block(args):
        qblk, q0 = args
        sc = jnp.einsum('bhiqd,bhikd->bhiqk', qblk, kt).astype(jnp.float32) * scale
        qpos = q0 + jnp.arange(BLOCK, dtype=jnp.int32)
        dist = jnp.abs(qpos[:, None] - kpos[None, :]).astype(jnp.float32)
        p = jax.nn.softmax(sc - sl * dist, axis=-1)
        a = p[:, :, 0] - lam * p[:, :, 1]
        return jnp.einsum('bhqk,bhkd->bhqd', a.astype(vt.dtype), vt)

    starts = jnp.arange(nb, dtype=jnp.int32) * BLOCK
    o = lax.map(block, (qb, starts))
    o = o.transpose(1, 0, 3, 2, 4).reshape(b, s, B_HEADS, B_V_DIM)
    o = (_rmsnorm(o, subln_g) * (1.0 - lambda_init)).astype(v.dtype)
    return o.reshape(b, s, B_HEADS * B_V_DIM)


def _encoder(x, norm_attn_g, w_in, sink_logits, lambda_q1, lambda_k1, lambda_q2, lambda_k2,
             diff_subln_g, w_out, norm_mlp_g, w_up, w_down, norm_final_g):
    b, s = x.shape[0], x.shape[1]
    slopes_a, slopes_b = _alibi_slopes()
    for l in range(DEPTH):
        h = _rmsnorm(x, norm_attn_g[l])
        proj = h @ w_in[l]
        aq, ak, av, bq, bk, bv = jnp.split(proj, SPLITS, axis=-1)
        oa = _window_gqa_sink(aq.reshape(b, s, A_HEADS, HEAD_DIM),
                              ak.reshape(b, s, A_KV_HEADS, HEAD_DIM),
                              av.reshape(b, s, A_KV_HEADS, HEAD_DIM),
                              sink_logits[l], slopes_a)
        lambda_init = 0.8 - 0.6 * math.exp(-0.3 * l)
        lam = (jnp.exp(jnp.sum(lambda_q1[l].astype(jnp.float32) * lambda_k1[l].astype(jnp.float32)))
               - jnp.exp(jnp.sum(lambda_q2[l].astype(jnp.float32) * lambda_k2[l].astype(jnp.float32)))
               + lambda_init)
        ob = _diff_attention(bq.reshape(b, s, B_HEADS, 2, B_QK_DIM),
                             bk.reshape(b, s, B_HEADS, 2, B_QK_DIM),
                             bv.reshape(b, s, B_HEADS, B_V_DIM),
                             lam, slopes_b, diff_subln_g[l], lambda_init)
        x = x + jnp.concatenate([oa, ob], axis=-1) @ w_out[l]
        h = _rmsnorm(x, norm_mlp_g[l])
        u = jax.nn.relu(h @ w_up[l])
        x = x + (u * u) @ w_down[l]
    return _rmsnorm(x, norm_final_g)


def setup_inputs(seed: int = 0) -> dict:
    key = jax.random.key(seed)
    ks = jax.random.split(key, 16)
    f32 = jnp.float32
    nrm = lambda k, shape, sc: jax.random.normal(k, shape, f32) * sc
    return {
        "x_prompt": nrm(ks[0], (BATCH, SEQ, D_MODEL), 1.0),
        "x_sample": nrm(ks[1], (DEC_BATCH, DEC_SEQ, D_MODEL), 1.0),
        "norm_attn_g": 1.0 + nrm(ks[2], (DEPTH, D_MODEL), 0.02),
        "w_in": nrm(ks[3], (DEPTH, D_MODEL, IN_WIDTH), D_MODEL ** -0.5),
        "sink_logits": nrm(ks[4], (DEPTH, A_HEADS), 0.5),
        "lambda_q1": nrm(ks[5], (DEPTH, B_QK_DIM), 0.1),
        "lambda_k1": nrm(ks[6], (DEPTH, B_QK_DIM), 0.1),
        "lambda_q2": nrm(ks[7], (DEPTH, B_QK_DIM), 0.1),
        "lambda_k2": nrm(ks[8], (DEPTH, B_QK_DIM), 0.1),
        "diff_subln_g": 1.0 + nrm(ks[9], (DEPTH, B_V_DIM), 0.02),
        "w_out": nrm(ks[10], (DEPTH, MIX_WIDTH, D_MODEL), MIX_WIDTH ** -0.5),
        "norm_mlp_g": 1.0 + nrm(ks[11], (DEPTH, D_MODEL), 0.02),
        "w_up": nrm(ks[12], (DEPTH, D_MODEL, D_FF), D_MODEL ** -0.5),
        "w_down": nrm(ks[13], (DEPTH, D_FF, D_MODEL), D_FF ** -0.5),
        "norm_final_g": 1.0 + nrm(ks[14], (D_MODEL,), 0.02),
    }


def reference(x_prompt, x_sample, norm_attn_g, w_in, sink_logits, lambda_q1, lambda_k1,
              lambda_q2, lambda_k2, diff_subln_g, w_out, norm_mlp_g, w_up, w_down, norm_final_g):
    y_prompt = _encoder(x_prompt, norm_attn_g, w_in, sink_logits, lambda_q1, lambda_k1, lambda_q2,
                        lambda_k2, diff_subln_g, w_out, norm_mlp_g, w_up, w_down, norm_final_g)
    y_sample = _encoder(x_sample, norm_attn_g, w_in, sink_logits, lambda_q1, lambda_k1, lambda_q2,
                        lambda_k2, diff_subln_g, w_out, norm_mlp_g, w_up, w_down, norm_final_g)
    return (y_prompt, y_sample)
```

```python
import functools
import math

import jax
import jax.numpy as jnp
from jax import lax
from jax.experimental import pallas as pl
from jax.experimental.pallas import tpu as pltpu

D_MODEL = 2048
HEAD_DIM = 128
N_HEADS_TOTAL = D_MODEL // HEAD_DIM
A_HEADS = N_HEADS_TOTAL // 2
A_KV_HEADS = 2
A_GROUP = A_HEADS // A_KV_HEADS
WINDOW = 128
BLOCK = 128
B_HEADS = N_HEADS_TOTAL - A_HEADS
B_QK_DIM = HEAD_DIM // 2
B_V_DIM = HEAD_DIM
MIX_WIDTH = A_HEADS * HEAD_DIM + B_HEADS * B_V_DIM
D_FF = 4 * D_MODEL
EPS = 1e-5
A_Q_W = A_HEADS * HEAD_DIM
A_KV_W = A_KV_HEADS * HEAD_DIM
B_QK_W = B_HEADS * 2 * B_QK_DIM
B_V_W = B_HEADS * B_V_DIM
IN_WIDTH = A_Q_W + 2 * A_KV_W + 2 * B_QK_W + B_V_W
LAMBDA_INIT = 0.8 - 0.6 * math.exp(-0.3 * 0)

_AK_BLK = A_Q_W // HEAD_DIM
_AV_BLK = (A_Q_W + A_KV_W) // HEAD_DIM
_BQ_BLK = (A_Q_W + 2 * A_KV_W) // HEAD_DIM
_BK_BLK = _BQ_BLK + B_QK_W // HEAD_DIM
_BV_BLK = _BK_BLK + B_QK_W // HEAD_DIM

_VMEM_LIMIT_BYTES = 56 * 1024 * 1024

_NT = (((1,), (1,)), ((), ()))
_TN = (((0,), (0,)), ((), ()))

_BF16 = jnp.bfloat16
_F32 = jnp.float32


def _rms_scale(x):
    return x * lax.rsqrt(jnp.mean(x * x, axis=-1, keepdims=True) + EPS)


def _inproj_body(x_ref, g_ref, w_ref, cs_ref, o_ref, h_ref):
    @pl.when(pl.program_id(1) == 0)
    def _():
        h_ref[...] = (_rms_scale(x_ref[...]) * g_ref[...]).astype(_BF16)

    acc = jnp.dot(h_ref[...], w_ref[...], preferred_element_type=_F32)
    o_ref[...] = (acc * cs_ref[...]).astype(o_ref.dtype)


def _inproj(x2d, g, w, colscale, *, tm=1024, tn=1536):
    t = x2d.shape[0]
    return pl.pallas_call(
        _inproj_body,
        grid=(t // tm, IN_WIDTH // tn),
        in_specs=[
            pl.BlockSpec((tm, D_MODEL), lambda i, j: (i, 0)),
            pl.BlockSpec((1, D_MODEL), lambda i, j: (0, 0)),
            pl.BlockSpec((D_MODEL, tn), lambda i, j: (0, j)),
            pl.BlockSpec((1, tn), lambda i, j: (0, j)),
        ],
        out_specs=pl.BlockSpec((tm, tn), lambda i, j: (i, j)),
        out_shape=jax.ShapeDtypeStruct((t, IN_WIDTH), _BF16),
        scratch_shapes=[pltpu.VMEM((tm, D_MODEL), _BF16)],
        compiler_params=pltpu.CompilerParams(
            dimension_semantics=("parallel", "arbitrary"),
            vmem_limit_bytes=_VMEM_LIMIT_BYTES),
        name="inproj",
    )(x2d, g, w, colscale)


def _winattn_body(par_ref, q_ref, kp_ref, kc_ref, kn_ref, vp_ref, vc_ref, vn_ref, o_ref, *, seq):
    n = pl.program_id(1)
    kv = pl.program_id(2)
    q = jnp.concatenate([q_ref[:, g * HEAD_DIM:(g + 1) * HEAD_DIM] for g in range(A_GROUP)], axis=0)
    kcat = jnp.concatenate([kp_ref[...], kc_ref[...], kn_ref[...]], axis=0)
    vcat = jnp.concatenate([vp_ref[...], vc_ref[...], vn_ref[...]], axis=0)
    s = lax.dot_general(q, kcat, _NT, preferred_element_type=_F32) * (1.0 / math.sqrt(HEAD_DIM))
    qi = lax.broadcasted_iota(jnp.int32, (BLOCK, 3 * BLOCK), 0)
    kj = lax.broadcasted_iota(jnp.int32, (BLOCK, 3 * BLOCK), 1)
    dist = jnp.abs(kj - BLOCK - qi)
    kpos = (n - 1) * BLOCK + kj
    valid = (dist <= WINDOW) & (kpos >= 0) & (kpos < seq)
    distf = dist.astype(_F32)
    ps = []
    for g in range(A_GROUP):
        slope = par_ref[kv * A_GROUP + g]
        sink = par_ref[A_HEADS + kv * A_GROUP + g]
        sg = s[g * BLOCK:(g + 1) * BLOCK]
        sg = jnp.where(valid, sg - slope * distf, -jnp.inf)
        m = jnp.maximum(jnp.max(sg, axis=-1, keepdims=True), sink)
        p = jnp.exp(sg - m)
        den = jnp.sum(p, axis=-1, keepdims=True) + jnp.exp(sink - m)
        ps.append((p / den).astype(_BF16))
    o = jnp.dot(jnp.concatenate(ps, axis=0), vcat, preferred_element_type=_F32)
    for g in range(A_GROUP):
        o_ref[:, g * HEAD_DIM:(g + 1) * HEAD_DIM] = o[g * BLOCK:(g + 1) * BLOCK].astype(o_ref.dtype)


def _winattn(proj, params_a, *, batch, seq):
    t = batch * seq
    nb = seq // BLOCK
    qw = A_GROUP * HEAD_DIM

    def row(b, n):
        return b * nb + n

    def kv_spec(col0, shift):
        def imap(b, n, kv):
            return (row(b, jnp.clip(n + shift, 0, nb - 1)), col0 + kv)
        return pl.BlockSpec((BLOCK, HEAD_DIM), imap)

    return pl.pallas_call(
        functools.partial(_winattn_body, seq=seq),
        grid=(batch, nb, A_KV_HEADS),
        in_specs=[
            pl.BlockSpec(memory_space=pltpu.SMEM),
            pl.BlockSpec((BLOCK, qw), lambda b, n, kv: (row(b, n), kv)),
            kv_spec(_AK_BLK, -1), kv_spec(_AK_BLK, 0), kv_spec(_AK_BLK, 1),
            kv_spec(_AV_BLK, -1), kv_spec(_AV_BLK, 0), kv_spec(_AV_BLK, 1),
        ],
        out_specs=pl.BlockSpec((BLOCK, qw), lambda b, n, kv: (row(b, n), kv)),
        out_shape=jax.ShapeDtypeStruct((t, A_Q_W), _BF16),
        compiler_params=pltpu.CompilerParams(
            dimension_semantics=("parallel", "parallel", "parallel")),
        name="winattn",
    )(params_a, proj, proj, proj, proj, proj, proj, proj)


def _diffattn_body(slope_ref, q_ref, k_ref, v_ref, lq1_ref, lk1_ref, lq2_ref, lk2_ref, g_ref, o_ref,
                   qm_ref, m_ref, l_ref, acc_ref, *, tq, tk, seq):
    h = pl.program_id(1)
    qi = pl.program_id(2)
    slope = slope_ref[h]

    q = q_ref[...]
    lane = lax.broadcasted_iota(jnp.int32, (tq, HEAD_DIM), 1)
    zero = jnp.zeros_like(q)
    qm_ref[0:tq, :] = jnp.where(lane < B_QK_DIM, q, zero)
    qm_ref[tq:2 * tq, :] = jnp.where(lane >= B_QK_DIM, q, zero)
    m_ref[...] = jnp.full_like(m_ref, -jnp.inf)
    l_ref[...] = jnp.zeros_like(l_ref)
    acc_ref[...] = jnp.zeros_like(acc_ref)

    krow = lax.broadcasted_iota(jnp.int32, (tk, tq), 0)
    qcol = lax.broadcasted_iota(jnp.int32, (tk, tq), 1)
    rel = qcol - krow

    def body(ki, carry):
        k0 = pl.multiple_of(ki * tk, tk)
        k = k_ref[pl.ds(k0, tk), :]
        v = v_ref[pl.ds(k0, tk), :]
        st = lax.dot_general(k, qm_ref[...], _NT, preferred_element_type=_F32)
        dist = jnp.abs(rel + (qi * tq - ki * tk)).astype(_F32)
        bias = slope * dist
        s = st - jnp.concatenate([bias, bias], axis=1)
        m_old = m_ref[...]
        m_new = jnp.maximum(m_old, jnp.max(s, axis=0, keepdims=True))
        alpha = jnp.exp(m_old - m_new)
        p = jnp.exp(s - m_new)
        l_ref[...] = alpha * l_ref[...] + jnp.sum(p, axis=0, keepdims=True)
        pv = lax.dot_general(v, p.astype(_BF16), _TN, preferred_element_type=_F32)
        acc_ref[...] = alpha * acc_ref[...] + pv
        m_ref[...] = m_new
        return carry

    lax.fori_loop(0, seq // tk, body, 0)

    lam = (jnp.exp(jnp.sum(lq1_ref[...] * lk1_ref[...], axis=-1, keepdims=True))
           - jnp.exp(jnp.sum(lq2_ref[...] * lk2_ref[...], axis=-1, keepdims=True))
           + LAMBDA_INIT)
    on = acc_ref[...] / l_ref[...]
    o = on[:, 0:tq] - lam * on[:, tq:2 * tq]
    y = o * lax.rsqrt(jnp.mean(o * o, axis=0, keepdims=True) + EPS)
    o_ref[...] = ((y.T * g_ref[...]) * (1.0 - LAMBDA_INIT)).astype(o_ref.dtype)


def _diffattn(proj, slopes_b, lq1, lk1, lq2, lk2, subln_g, *, batch, seq, tq=256, tk=512):
    t = batch * seq
    nq = seq // tq
    small = lambda shape: pl.BlockSpec(shape, lambda b, h, i: (0, 0))
    return pl.pallas_call(
        functools.partial(_diffattn_body, tq=tq, tk=tk, seq=seq),
        grid=(batch, B_HEADS, nq),
        in_specs=[
            pl.BlockSpec(memory_space=pltpu.SMEM),
            pl.BlockSpec((tq, HEAD_DIM), lambda b, h, i: (b * nq + i, _BQ_BLK + h)),
            pl.BlockSpec((seq, HEAD_DIM), lambda b, h, i: (b, _BK_BLK + h)),
            pl.BlockSpec((seq, HEAD_DIM), lambda b, h, i: (b, _BV_BLK + h)),
            small((1, B_QK_DIM)), small((1, B_QK_DIM)), small((1, B_QK_DIM)), small((1, B_QK_DIM)),
            small((1, B_V_DIM)),
        ],
        out_specs=pl.BlockSpec((tq, B_V_DIM), lambda b, h, i: (b * nq + i, h)),
        out_shape=jax.ShapeDtypeStruct((t, B_V_W), _BF16),
        scratch_shapes=[
            pltpu.VMEM((2 * tq, HEAD_DIM), _BF16),
            pltpu.VMEM((1, 2 * tq), _F32),
            pltpu.VMEM((1, 2 * tq), _F32),
            pltpu.VMEM((B_V_DIM, 2 * tq), _F32),
        ],
        compiler_params=pltpu.CompilerParams(
            dimension_semantics=("parallel", "parallel", "parallel"),
            vmem_limit_bytes=_VMEM_LIMIT_BYTES),
        name="diffattn",
    )(slopes_b, proj, proj, proj, lq1, lk1, lq2, lk2, subln_g)


def _outproj_body(oa_ref, ob_ref, w_ref, x_ref, o_ref):
    a = jnp.concatenate([oa_ref[...], ob_ref[...]], axis=1)
    o_ref[...] = x_ref[...] + jnp.dot(a, w_ref[...], preferred_element_type=_F32)


def _outproj(oa, ob, w, x2d, *, tm=512):
    t = x2d.shape[0]
    return pl.pallas_call(
        _outproj_body,
        grid=(t // tm,),
        in_specs=[
            pl.BlockSpec((tm, A_Q_W), lambda i: (i, 0)),
            pl.BlockSpec((tm, B_V_W), lambda i: (i, 0)),
            pl.BlockSpec((MIX_WIDTH, D_MODEL), lambda i: (0, 0)),
            pl.BlockSpec((tm, D_MODEL), lambda i: (i, 0)),
        ],
        out_specs=pl.BlockSpec((tm, D_MODEL), lambda i: (i, 0)),
        out_shape=jax.ShapeDtypeStruct((t, D_MODEL), _F32),
        compiler_params=pltpu.CompilerParams(
            dimension_semantics=("parallel",),
            vmem_limit_bytes=_VMEM_LIMIT_BYTES),
        name="outproj",
    )(oa, ob, w, x2d)


def _mlp_body(x_ref, gm_ref, wu_ref, wd_ref, gf_ref, o_ref, h_ref, acc_ref):
    j = pl.program_id(1)

    @pl.when(j == 0)
    def _():
        x = x_ref[...]
        h_ref[...] = (_rms_scale(x) * gm_ref[...]).astype(_BF16)
        acc_ref[...] = x

    u = jnp.maximum(jnp.dot(h_ref[...], wu_ref[...], preferred_element_type=_F32), 0.0)
    acc_ref[...] += jnp.dot((u * u).astype(_BF16), wd_ref[...], preferred_element_type=_F32)

    @pl.when(j == pl.num_programs(1) - 1)
    def _():
        o_ref[...] = _rms_scale(acc_ref[...]) * gf_ref[...]


def _mlp(x1, g_mlp, w_up, w_down, g_final, *, tm=512, tf=1024):
    t = x1.shape[0]
    return pl.pallas_call(
        _mlp_body,
        grid=(t // tm, D_FF // tf),
        in_specs=[
            pl.BlockSpec((tm, D_MODEL), lambda i, j: (i, 0)),
            pl.BlockSpec((1, D_MODEL), lambda i, j: (0, 0)),
            pl.BlockSpec((D_MODEL, tf), lambda i, j: (0, j)),
            pl.BlockSpec((tf, D_MODEL), lambda i, j: (j, 0)),
            pl.BlockSpec((1, D_MODEL), lambda i, j: (0, 0)),
        ],
        out_specs=pl.BlockSpec((tm, D_MODEL), lambda i, j: (i, 0)),
        out_shape=jax.ShapeDtypeStruct((t, D_MODEL), _F32),
        scratch_shapes=[pltpu.VMEM((tm, D_MODEL), _BF16), pltpu.VMEM((tm, D_MODEL), _F32)],
        compiler_params=pltpu.CompilerParams(
            dimension_semantics=("parallel", "arbitrary"),
            vmem_limit_bytes=_VMEM_LIMIT_BYTES),
        name="mlp",
    )(x1, g_mlp, w_up, w_down, g_final)


def _alibi_slopes():
    i = jnp.arange(1, N_HEADS_TOTAL + 1, dtype=_F32)
    s = jnp.exp2(-8.0 / N_HEADS_TOTAL * i)
    return s[0::2], s[1::2]


def _encoder(x, p):
    batch, seq = x.shape[0], x.shape[1]
    x2d = x.reshape(batch * seq, D_MODEL)
    proj = _inproj(x2d, p["g_attn"], p["w_in"], p["colscale"])
    oa = _winattn(proj, p["params_a"], batch=batch, seq=seq)
    ob = _diffattn(proj, p["slopes_b"], p["lq1"], p["lk1"], p["lq2"], p["lk2"], p["subln_g"],
                   batch=batch, seq=seq)
    x1 = _outproj(oa, ob, p["w_out"], x2d)
    y = _mlp(x1, p["g_mlp"], p["w_up"], p["w_down"], p["g_final"])
    return y.reshape(batch, seq, D_MODEL)


def kernel(x_prompt, x_sample, norm_attn_g, w_in, sink_logits, lambda_q1, lambda_k1, lambda_q2, lambda_k2,
           diff_subln_g, w_out, norm_mlp_g, w_up, w_down, norm_final_g):
    slopes_a, slopes_b = _alibi_slopes()
    col = jnp.arange(IN_WIDTH)
    in_bq = (col >= _BQ_BLK * HEAD_DIM) & (col < _BK_BLK * HEAD_DIM)
    p = {
        "g_attn": norm_attn_g[0].reshape(1, D_MODEL).astype(_F32),
        "w_in": w_in[0].astype(_BF16),
        "colscale": jnp.where(in_bq, 1.0 / math.sqrt(B_QK_DIM), 1.0).astype(_F32).reshape(1, IN_WIDTH),
        "params_a": jnp.concatenate([slopes_a, sink_logits[0].astype(_F32)]),
        "slopes_b": slopes_b,
        "lq1": lambda_q1[0].reshape(1, B_QK_DIM).astype(_F32),
        "lk1": lambda_k1[0].reshape(1, B_QK_DIM).astype(_F32),
        "lq2": lambda_q2[0].reshape(1, B_QK_DIM).astype(_F32),
        "lk2": lambda_k2[0].reshape(1, B_QK_DIM).astype(_F32),
        "subln_g": diff_subln_g[0].reshape(1, B_V_DIM).astype(_F32),
        "w_out": w_out[0].astype(_BF16),
        "g_mlp": norm_mlp_g[0].reshape(1, D_MODEL).astype(_F32),
        "w_up": w_up[0].astype(_BF16),
        "w_down": w_down[0].astype(_BF16),
        "g_final": norm_final_g.reshape(1, D_MODEL).astype(_F32),
    }
    return (_encoder(x_prompt, p), _encoder(x_sample, p))
```

```python
import functools
import math

import jax
import jax.numpy as jnp
from jax import lax
from jax.experimental import pallas as pl
from jax.experimental.pallas import tpu as pltpu

D_MODEL = 2048
HEAD_DIM = 128
N_HEADS_TOTAL = D_MODEL // HEAD_DIM
A_HEADS = N_HEADS_TOTAL // 2
A_KV_HEADS = 2
A_GROUP = A_HEADS // A_KV_HEADS
WINDOW = 128
BLOCK = 128
B_HEADS = N_HEADS_TOTAL - A_HEADS
B_QK_DIM = HEAD_DIM // 2
B_V_DIM = HEAD_DIM
MIX_WIDTH = A_HEADS * HEAD_DIM + B_HEADS * B_V_DIM
D_FF = 4 * D_MODEL
EPS = 1e-5
A_Q_W = A_HEADS * HEAD_DIM
A_KV_W = A_KV_HEADS * HEAD_DIM
B_QK_W = B_HEADS * 2 * B_QK_DIM
B_V_W = B_HEADS * B_V_DIM
IN_WIDTH = A_Q_W + 2 * A_KV_W + 2 * B_QK_W + B_V_W
LAMBDA_INIT = 0.8 - 0.6 * math.exp(-0.3 * 0)

_AK_BLK = A_Q_W // HEAD_DIM
_AV_BLK = (A_Q_W + A_KV_W) // HEAD_DIM
_BQ_BLK = (A_Q_W + 2 * A_KV_W) // HEAD_DIM
_BK_BLK = _BQ_BLK + B_QK_W // HEAD_DIM
_BV_BLK = _BK_BLK + B_QK_W // HEAD_DIM

_VMEM_LIMIT_BYTES = 56 * 1024 * 1024

_NT = (((1,), (1,)), ((), ()))
_TN = (((0,), (0,)), ((), ()))

_BF16 = jnp.bfloat16
_F32 = jnp.float32


def _rms_scale(x):
    return x * lax.rsqrt(jnp.mean(x * x, axis=-1, keepdims=True) + EPS)


def _inproj_body(x_ref, g_ref, w_ref, cs_ref, o_ref, h_ref):
    @pl.when(pl.program_id(1) == 0)
    def _():
        h_ref[...] = (_rms_scale(x_ref[...]) * g_ref[...]).astype(_BF16)

    acc = jnp.dot(h_ref[...], w_ref[...], preferred_element_type=_F32)
    o_ref[...] = (acc * cs_ref[...]).astype(o_ref.dtype)


def _inproj(x2d, g, w, colscale, *, tm=1024, tn=1536):
    t = x2d.shape[0]
    return pl.pallas_call(
        _inproj_body,
        grid=(t // tm, IN_WIDTH // tn),
        in_specs=[
            pl.BlockSpec((tm, D_MODEL), lambda i, j: (i, 0)),
            pl.BlockSpec((1, D_MODEL), lambda i, j: (0, 0)),
            pl.BlockSpec((D_MODEL, tn), lambda i, j: (0, j)),
            pl.BlockSpec((1, tn), lambda i, j: (0, j)),
        ],
        out_specs=pl.BlockSpec((tm, tn), lambda i, j: (i, j)),
        out_shape=jax.ShapeDtypeStruct((t, IN_WIDTH), _BF16),
        scratch_shapes=[pltpu.VMEM((tm, D_MODEL), _BF16)],
        compiler_params=pltpu.CompilerParams(
            dimension_semantics=("parallel", "arbitrary"),
            vmem_limit_bytes=_VMEM_LIMIT_BYTES),
        name="inproj",
    )(x2d, g, w, colscale)


def _winattn_body(par_ref, q_ref, kp_ref, kc_ref, kn_ref, vp_ref, vc_ref, vn_ref, o_ref, *, seq):
    n = pl.program_id(1)
    kv = pl.program_id(2)
    q = jnp.concatenate([q_ref[:, g * HEAD_DIM:(g + 1) * HEAD_DIM] for g in range(A_GROUP)], axis=0)
    kcat = jnp.concatenate([kp_ref[...], kc_ref[...], kn_ref[...]], axis=0)
    vcat = jnp.concatenate([vp_ref[...], vc_ref[...], vn_ref[...]], axis=0)
    s = lax.dot_general(q, kcat, _NT, preferred_element_type=_F32) * (1.0 / math.sqrt(HEAD_DIM))
    qi = lax.broadcasted_iota(jnp.int32, (BLOCK, 3 * BLOCK), 0)
    kj = lax.broadcasted_iota(jnp.int32, (BLOCK, 3 * BLOCK), 1)
    dist = jnp.abs(kj - BLOCK - qi)
    kpos = (n - 1) * BLOCK + kj
    valid = (dist <= WINDOW) & (kpos >= 0) & (kpos < seq)
    distf = dist.astype(_F32)
    ps = []
    for g in range(A_GROUP):
        slope = par_ref[kv * A_GROUP + g]
        sink = par_ref[A_HEADS + kv * A_GROUP + g]
        sg = s[g * BLOCK:(g + 1) * BLOCK]
        sg = jnp.where(valid, sg - slope * distf, -jnp.inf)
        m = jnp.maximum(jnp.max(sg, axis=-1, keepdims=True), sink)
        p = jnp.exp(sg - m)
        den = jnp.sum(p, axis=-1, keepdims=True) + jnp.exp(sink - m)
        ps.append((p / den).astype(_BF16))
    o = jnp.dot(jnp.concatenate(ps, axis=0), vcat, preferred_element_type=_F32)
    for g in range(A_GROUP):
        o_ref[:, g * HEAD_DIM:(g + 1) * HEAD_DIM] = o[g * BLOCK:(g + 1) * BLOCK].astype(o_ref.dtype)


def _winattn(proj, params_a, *, batch, seq):
    t = batch * seq
    nb = seq // BLOCK
    qw = A_GROUP * HEAD_DIM

    def row(b, n):
        return b * nb + n

    def kv_spec(col0, shift):
        def imap(b, n, kv):
            return (row(b, jnp.clip(n + shift, 0, nb - 1)), col0 + kv)
        return pl.BlockSpec((BLOCK, HEAD_DIM), imap)

    return pl.pallas_call(
        functools.partial(_winattn_body, seq=seq),
        grid=(batch, nb, A_KV_HEADS),
        in_specs=[
            pl.BlockSpec(memory_space=pltpu.SMEM),
            pl.BlockSpec((BLOCK, qw), lambda b, n, kv: (row(b, n), kv)),
            kv_spec(_AK_BLK, -1), kv_spec(_AK_BLK, 0), kv_spec(_AK_BLK, 1),
            kv_spec(_AV_BLK, -1), kv_spec(_AV_BLK, 0), kv_spec(_AV_BLK, 1),
        ],
        out_specs=pl.BlockSpec((BLOCK, qw), lambda b, n, kv: (row(b, n), kv)),
        out_shape=jax.ShapeDtypeStruct((t, A_Q_W), _BF16),
        compiler_params=pltpu.CompilerParams(
            dimension_semantics=("parallel", "parallel", "parallel")),
        name="winattn",
    )(params_a, proj, proj, proj, proj, proj, proj, proj)


_ONES_ROWS = 16
_N_SPLIT = 3


def _split_bf16(t):
    pieces = []
    for _ in range(_N_SPLIT):
        piece = t.astype(_BF16)
        pieces.append(piece)
        t = t - piece.astype(_F32)
    return pieces


def _diffattn_body(slope_ref, q_ref, k_ref, v_ref, lq1_ref, lk1_ref, lq2_ref, lk2_ref, g_ref, o_ref,
                   qm_ref, kf_ref, vt_ref, a0_ref, a1_ref, m_ref, acc_ref, *, tq, tk, seq, unroll):
    h = pl.program_id(1)
    qi = pl.program_id(2)
    nk = seq // tk
    slope = slope_ref[h]

    @pl.when(qi == 0)
    def _():
        r = lax.broadcasted_iota(jnp.int32, (tk, HEAD_DIM), 0).astype(_F32)
        lane = lax.broadcasted_iota(jnp.int32, (tk, HEAD_DIM), 1)
        f = jnp.zeros((tk, HEAD_DIM), _F32)
        for c, piece in enumerate(_split_bf16(slope * r)):
            f = jnp.where(lane == c, piece.astype(_F32), f)
        kf_ref[...] = f.astype(_BF16)
        ones = jnp.ones((_ONES_ROWS, tk), _BF16)
        for t in range(nk):
            vt_ref[t] = jnp.concatenate([v_ref[t * tk:(t + 1) * tk, :].T, ones], axis=0)

    q = q_ref[...]
    lane = lax.broadcasted_iota(jnp.int32, (tq, HEAD_DIM), 1)
    zero = jnp.zeros_like(q)
    qmask = jnp.concatenate([jnp.where(lane < B_QK_DIM, q, zero), jnp.where(lane >= B_QK_DIM, q, zero)], axis=0)
    lane2 = lax.broadcasted_iota(jnp.int32, (2 * tq, HEAD_DIM), 1)
    gpos = jnp.where(lane2 < _N_SPLIT, 1.0, 0.0).astype(_BF16)
    qm_ref[0] = jnp.concatenate([qmask, gpos], axis=1)
    qm_ref[1] = jnp.concatenate([qmask, -gpos], axis=1)
    m_ref[...] = jnp.full_like(m_ref, -jnp.inf)
    acc_ref[...] = jnp.zeros_like(acc_ref)

    col = lax.broadcasted_iota(jnp.int32, (1, 2 * tq), 1)
    qpos = (qi * tq + jnp.where(col < tq, col, col - tq)).astype(_F32)

    def update(a, c, vt):
        m_old = m_ref[...]
        m_new = jnp.maximum(m_old, jnp.max(a, axis=0, keepdims=True) + c)
        alpha = jnp.exp(m_old - m_new)
        p = jnp.exp(a - (m_new - c))
        pv = jnp.dot(vt, p.astype(_BF16), preferred_element_type=_F32)
        acc_ref[...] = alpha * acc_ref[...] + pv
        m_ref[...] = m_new

    kd = (qi * tq) // tk

    def far_index(t):
        ki = t + (t >= kd).astype(jnp.int32)
        return ki, (ki > kd).astype(jnp.int32), pl.multiple_of(ki * tk, tk)

    def far_scores(t, a_ref):
        _, after, k0 = far_index(t)
        ka = jnp.concatenate([k_ref[pl.ds(k0, tk), :], kf_ref[...]], axis=1)
        a_ref[...] = lax.dot_general(ka, qm_ref[after], _NT, preferred_element_type=_F32)

    def far_fold(t, a_ref):
        ki, after, k0 = far_index(t)
        sign = (1 - 2 * after).astype(_F32)
        c = (sign * slope) * (k0.astype(_F32) - qpos)
        update(a_ref[...], c, vt_ref[ki])

    far_scores(jnp.int32(0), a0_ref)

    def far_pair(j, carry):
        t = 2 * j
        far_scores(t + 1, a1_ref)
        far_fold(t, a0_ref)
        far_scores(t + 2, a0_ref)
        far_fold(t + 1, a1_ref)
        return carry

    lax.fori_loop(0, (nk - 2) // 2, far_pair, 0, unroll=unroll)
    far_fold(jnp.int32(nk - 2), a0_ref)

    k0 = pl.multiple_of(kd * tk, tk)
    st = lax.dot_general(k_ref[pl.ds(k0, tk), :], qm_ref[0, :, 0:HEAD_DIM], _NT, preferred_element_type=_F32)
    krow = lax.broadcasted_iota(jnp.int32, (tk, tq), 0)
    qcol = lax.broadcasted_iota(jnp.int32, (tk, tq), 1)
    bias = slope * jnp.abs(qcol - krow + (qi * tq - kd * tk)).astype(_F32)
    update(st - jnp.concatenate([bias, bias], axis=1), jnp.zeros((1, 2 * tq), _F32), vt_ref[kd])

    lam = (jnp.exp(jnp.sum(lq1_ref[...] * lk1_ref[...], axis=-1, keepdims=True))
           - jnp.exp(jnp.sum(lq2_ref[...] * lk2_ref[...], axis=-1, keepdims=True))
           + LAMBDA_INIT)
    acc = acc_ref[...]
    on = acc[0:B_V_DIM] / acc[B_V_DIM:B_V_DIM + 1]
    o = on[:, 0:tq] - lam * on[:, tq:2 * tq]
    y = o * lax.rsqrt(jnp.mean(o * o, axis=0, keepdims=True) + EPS)
    o_ref[...] = ((y.T * g_ref[...]) * (1.0 - LAMBDA_INIT)).astype(o_ref.dtype)


def _diffattn(proj, slopes_b, lq1, lk1, lq2, lk2, subln_g, *, batch, seq, tq=256, tk=512, unroll=1):
    t = batch * seq
    nq = seq // tq
    nk = seq // tk
    assert tk % tq == 0 and seq % tk == 0 and nk % 2 == 0 and nk >= 2
    small = lambda shape: pl.BlockSpec(shape, lambda b, h, i: (0, 0))
    return pl.pallas_call(
        functools.partial(_diffattn_body, tq=tq, tk=tk, seq=seq, unroll=unroll),
        grid=(batch, B_HEADS, nq),
        in_specs=[
            pl.BlockSpec(memory_space=pltpu.SMEM),
            pl.BlockSpec((tq, HEAD_DIM), lambda b, h, i: (b * nq + i, _BQ_BLK + h)),
            pl.BlockSpec((seq, HEAD_DIM), lambda b, h, i: (b, _BK_BLK + h)),
            pl.BlockSpec((seq, HEAD_DIM), lambda b, h, i: (b, _BV_BLK + h)),
            small((1, B_QK_DIM)), small((1, B_QK_DIM)), small((1, B_QK_DIM)), small((1, B_QK_DIM)),
            small((1, B_V_DIM)),
        ],
        out_specs=pl.BlockSpec((tq, B_V_DIM), lambda b, h, i: (b * nq + i, h)),
        out_shape=jax.ShapeDtypeStruct((t, B_V_W), _BF16),
        scratch_shapes=[
            pltpu.VMEM((2, 2 * tq, 2 * HEAD_DIM), _BF16),
            pltpu.VMEM((tk, HEAD_DIM), _BF16),
            pltpu.VMEM((nk, B_V_DIM + _ONES_ROWS, tk), _BF16),
            pltpu.VMEM((tk, 2 * tq), _F32),
            pltpu.VMEM((tk, 2 * tq), _F32),
            pltpu.VMEM((1, 2 * tq), _F32),
            pltpu.VMEM((B_V_DIM + _ONES_ROWS, 2 * tq), _F32),
        ],
        compiler_params=pltpu.CompilerParams(
            dimension_semantics=("parallel", "parallel", "arbitrary"),
            vmem_limit_bytes=_VMEM_LIMIT_BYTES),
        name="diffattn",
    )(slopes_b, proj, proj, proj, lq1, lk1, lq2, lk2, subln_g)


def _outproj_body(oa_ref, ob_ref, w_ref, x_ref, o_ref):
    a = jnp.concatenate([oa_ref[...], ob_ref[...]], axis=1)
    o_ref[...] = x_ref[...] + jnp.dot(a, w_ref[...], preferred_element_type=_F32)


def _outproj(oa, ob, w, x2d, *, tm=512):
    t = x2d.shape[0]
    return pl.pallas_call(
        _outproj_body,
        grid=(t // tm,),
        in_specs=[
            pl.BlockSpec((tm, A_Q_W), lambda i: (i, 0)),
            pl.BlockSpec((tm, B_V_W), lambda i: (i, 0)),
            pl.BlockSpec((MIX_WIDTH, D_MODEL), lambda i: (0, 0)),
            pl.BlockSpec((tm, D_MODEL), lambda i: (i, 0)),
        ],
        out_specs=pl.BlockSpec((tm, D_MODEL), lambda i: (i, 0)),
        out_shape=jax.ShapeDtypeStruct((t, D_MODEL), _F32),
        compiler_params=pltpu.CompilerParams(
            dimension_semantics=("parallel",),
            vmem_limit_bytes=_VMEM_LIMIT_BYTES),
        name="outproj",
    )(oa, ob, w, x2d)


def _mlp_body(x_ref, gm_ref, wu_ref, wd_ref, gf_ref, o_ref, h_ref, acc_ref):
    j = pl.program_id(1)

    @pl.when(j == 0)
    def _():
        x = x_ref[...]
        h_ref[...] = (_rms_scale(x) * gm_ref[...]).astype(_BF16)
        acc_ref[...] = x

    u = jnp.maximum(jnp.dot(h_ref[...], wu_ref[...], preferred_element_type=_F32), 0.0)
    acc_ref[...] += jnp.dot((u * u).astype(_BF16), wd_ref[...], preferred_element_type=_F32)

    @pl.when(j == pl.num_programs(1) - 1)
    def _():
        o_ref[...] = _rms_scale(acc_ref[...]) * gf_ref[...]


def _mlp(x1, g_mlp, w_up, w_down, g_final, *, tm=512, tf=1024):
    t = x1.shape[0]
    return pl.pallas_call(
        _mlp_body,
        grid=(t // tm, D_FF // tf),
        in_specs=[
            pl.BlockSpec((tm, D_MODEL), lambda i, j: (i, 0)),
            pl.BlockSpec((1, D_MODEL), lambda i, j: (0, 0)),
            pl.BlockSpec((D_MODEL, tf), lambda i, j: (0, j)),
            pl.BlockSpec((tf, D_MODEL), lambda i, j: (j, 0)),
            pl.BlockSpec((1, D_MODEL), lambda i, j: (0, 0)),
        ],
        out_specs=pl.BlockSpec((tm, D_MODEL), lambda i, j: (i, 0)),
        out_shape=jax.ShapeDtypeStruct((t, D_MODEL), _F32),
        scratch_shapes=[pltpu.VMEM((tm, D_MODEL), _BF16), pltpu.VMEM((tm, D_MODEL), _F32)],
        compiler_params=pltpu.CompilerParams(
            dimension_semantics=("parallel", "arbitrary"),
            vmem_limit_bytes=_VMEM_LIMIT_BYTES),
        name="mlp",
    )(x1, g_mlp, w_up, w_down, g_final)


def _alibi_slopes():
    i = jnp.arange(1, N_HEADS_TOTAL + 1, dtype=_F32)
    s = jnp.exp2(-8.0 / N_HEADS_TOTAL * i)
    return s[0::2], s[1::2]


def _encoder(x, p):
    batch, seq = x.shape[0], x.shape[1]
    x2d = x.reshape(batch * seq, D_MODEL)
    proj = _inproj(x2d, p["g_attn"], p["w_in"], p["colscale"])
    oa = _winattn(proj, p["params_a"], batch=batch, seq=seq)
    ob = _diffattn(proj, p["slopes_b"], p["lq1"], p["lk1"], p["lq2"], p["lk2"], p["subln_g"],
                   batch=batch, seq=seq)
    x1 = _outproj(oa, ob, p["w_out"], x2d)
    y = _mlp(x1, p["g_mlp"], p["w_up"], p["w_down"], p["g_final"])
    return y.reshape(batch, seq, D_MODEL)


def kernel(x_prompt, x_sample, norm_attn_g, w_in, sink_logits, lambda_q1, lambda_k1, lambda_q2, lambda_k2,
           diff_subln_g, w_out, norm_mlp_g, w_up, w_down, norm_final_g):
    slopes_a, slopes_b = _alibi_slopes()
    col = jnp.arange(IN_WIDTH)
    in_bq = (col >= _BQ_BLK * HEAD_DIM) & (col < _BK_BLK * HEAD_DIM)
    p = {
        "g_attn": norm_attn_g[0].reshape(1, D_MODEL).astype(_F32),
        "w_in": w_in[0].astype(_BF16),
        "colscale": jnp.where(in_bq, 1.0 / math.sqrt(B_QK_DIM), 1.0).astype(_F32).reshape(1, IN_WIDTH),
        "params_a": jnp.concatenate([slopes_a, sink_logits[0].astype(_F32)]),
        "slopes_b": slopes_b,
        "lq1": lambda_q1[0].reshape(1, B_QK_DIM).astype(_F32),
        "lk1": lambda_k1[0].reshape(1, B_QK_DIM).astype(_F32),
        "lq2": lambda_q2[0].reshape(1, B_QK_DIM).astype(_F32),
        "lk2": lambda_k2[0].reshape(1, B_QK_DIM).astype(_F32),
        "subln_g": diff_subln_g[0].reshape(1, B_V_DIM).astype(_F32),
        "w_out": w_out[0].astype(_BF16),
        "g_mlp": norm_mlp_g[0].reshape(1, D_MODEL).astype(_F32),
        "w_up": w_up[0].astype(_BF16),
        "w_down": w_down[0].astype(_BF16),
        "g_final": norm_final_g.reshape(1, D_MODEL).astype(_F32),
    }
    return (_encoder(x_prompt, p), _encoder(x_sample, p))
```

```python
import functools
import math

import jax
import jax.numpy as jnp
from jax import lax
from jax.experimental import pallas as pl
from jax.experimental.pallas import tpu as pltpu

D_MODEL = 2048
HEAD_DIM = 128
N_HEADS_TOTAL = D_MODEL // HEAD_DIM
A_HEADS = N_HEADS_TOTAL // 2
A_KV_HEADS = 2
A_GROUP = A_HEADS // A_KV_HEADS
WINDOW = 128
BLOCK = 128
B_HEADS = N_HEADS_TOTAL - A_HEADS
B_QK_DIM = HEAD_DIM // 2
B_V_DIM = HEAD_DIM
MIX_WIDTH = A_HEADS * HEAD_DIM + B_HEADS * B_V_DIM
D_FF = 4 * D_MODEL
EPS = 1e-5
A_Q_W = A_HEADS * HEAD_DIM
A_KV_W = A_KV_HEADS * HEAD_DIM
B_QK_W = B_HEADS * 2 * B_QK_DIM
B_V_W = B_HEADS * B_V_DIM
IN_WIDTH = A_Q_W + 2 * A_KV_W + 2 * B_QK_W + B_V_W
LAMBDA_INIT = 0.8 - 0.6 * math.exp(-0.3 * 0)

_AK_BLK = A_Q_W // HEAD_DIM
_AV_BLK = (A_Q_W + A_KV_W) // HEAD_DIM
_BQ_BLK = (A_Q_W + 2 * A_KV_W) // HEAD_DIM
_BK_BLK = _BQ_BLK + B_QK_W // HEAD_DIM
_BV_BLK = _BK_BLK + B_QK_W // HEAD_DIM

_VMEM_LIMIT_BYTES = 56 * 1024 * 1024

_NT = (((1,), (1,)), ((), ()))
_TN = (((0,), (0,)), ((), ()))

_BF16 = jnp.bfloat16
_F32 = jnp.float32


def _rms_scale(x):
    return x * lax.rsqrt(jnp.mean(x * x, axis=-1, keepdims=True) + EPS)


def _inproj_body(x_ref, g_ref, w_ref, cs_ref, o_ref, h_ref):
    @pl.when(pl.program_id(1) == 0)
    def _():
        h_ref[...] = (_rms_scale(x_ref[...]) * g_ref[...]).astype(_BF16)

    acc = jnp.dot(h_ref[...], w_ref[...], preferred_element_type=_F32)
    o_ref[...] = (acc * cs_ref[...]).astype(o_ref.dtype)


def _inproj(x2d, g, w, colscale, *, tm=1024, tn=1536):
    t = x2d.shape[0]
    return pl.pallas_call(
        _inproj_body,
        grid=(t // tm, IN_WIDTH // tn),
        in_specs=[
            pl.BlockSpec((tm, D_MODEL), lambda i, j: (i, 0)),
            pl.BlockSpec((1, D_MODEL), lambda i, j: (0, 0)),
            pl.BlockSpec((D_MODEL, tn), lambda i, j: (0, j)),
            pl.BlockSpec((1, tn), lambda i, j: (0, j)),
        ],
        out_specs=pl.BlockSpec((tm, tn), lambda i, j: (i, j)),
        out_shape=jax.ShapeDtypeStruct((t, IN_WIDTH), _BF16),
        scratch_shapes=[pltpu.VMEM((tm, D_MODEL), _BF16)],
        compiler_params=pltpu.CompilerParams(
            dimension_semantics=("parallel", "arbitrary"),
            vmem_limit_bytes=_VMEM_LIMIT_BYTES),
        name="inproj",
    )(x2d, g, w, colscale)


def _winattn_body(par_ref, q_ref, kp_ref, kc_ref, kn_ref, vp_ref, vc_ref, vn_ref, o_ref, *, seq):
    n = pl.program_id(1)
    kv = pl.program_id(2)
    q = jnp.concatenate([q_ref[:, g * HEAD_DIM:(g + 1) * HEAD_DIM] for g in range(A_GROUP)], axis=0)
    kcat = jnp.concatenate([kp_ref[...], kc_ref[...], kn_ref[...]], axis=0)
    vcat = jnp.concatenate([vp_ref[...], vc_ref[...], vn_ref[...]], axis=0)
    s = lax.dot_general(q, kcat, _NT, preferred_element_type=_F32) * (1.0 / math.sqrt(HEAD_DIM))
    qi = lax.broadcasted_iota(jnp.int32, (BLOCK, 3 * BLOCK), 0)
    kj = lax.broadcasted_iota(jnp.int32, (BLOCK, 3 * BLOCK), 1)
    dist = jnp.abs(kj - BLOCK - qi)
    kpos = (n - 1) * BLOCK + kj
    valid = (dist <= WINDOW) & (kpos >= 0) & (kpos < seq)
    distf = dist.astype(_F32)
    ps = []
    for g in range(A_GROUP):
        slope = par_ref[kv * A_GROUP + g]
        sink = par_ref[A_HEADS + kv * A_GROUP + g]
        sg = s[g * BLOCK:(g + 1) * BLOCK]
        sg = jnp.where(valid, sg - slope * distf, -jnp.inf)
        m = jnp.maximum(jnp.max(sg, axis=-1, keepdims=True), sink)
        p = jnp.exp(sg - m)
        den = jnp.sum(p, axis=-1, keepdims=True) + jnp.exp(sink - m)
        ps.append((p / den).astype(_BF16))
    o = jnp.dot(jnp.concatenate(ps, axis=0), vcat, preferred_element_type=_F32)
    for g in range(A_GROUP):
        o_ref[:, g * HEAD_DIM:(g + 1) * HEAD_DIM] = o[g * BLOCK:(g + 1) * BLOCK].astype(o_ref.dtype)


def _winattn(proj, params_a, *, batch, seq):
    t = batch * seq
    nb = seq // BLOCK
    qw = A_GROUP * HEAD_DIM

    def row(b, n):
        return b * nb + n

    def kv_spec(col0, shift):
        def imap(b, n, kv):
            return (row(b, jnp.clip(n + shift, 0, nb - 1)), col0 + kv)
        return pl.BlockSpec((BLOCK, HEAD_DIM), imap)

    return pl.pallas_call(
        functools.partial(_winattn_body, seq=seq),
        grid=(batch, nb, A_KV_HEADS),
        in_specs=[
            pl.BlockSpec(memory_space=pltpu.SMEM),
            pl.BlockSpec((BLOCK, qw), lambda b, n, kv: (row(b, n), kv)),
            kv_spec(_AK_BLK, -1), kv_spec(_AK_BLK, 0), kv_spec(_AK_BLK, 1),
            kv_spec(_AV_BLK, -1), kv_spec(_AV_BLK, 0), kv_spec(_AV_BLK, 1),
        ],
        out_specs=pl.BlockSpec((BLOCK, qw), lambda b, n, kv: (row(b, n), kv)),
        out_shape=jax.ShapeDtypeStruct((t, A_Q_W), _BF16),
        compiler_params=pltpu.CompilerParams(
            dimension_semantics=("parallel", "parallel", "parallel")),
        name="winattn",
    )(params_a, proj, proj, proj, proj, proj, proj, proj)


_ONES_ROWS = 16
_N_SPLIT = 3
_EXP_ZERO = 110.0
_NORM_SLACK = 2.1


def _split_bf16(t):
    pieces = []
    for _ in range(_N_SPLIT):
        piece = t.astype(_BF16)
        pieces.append(piece)
        t = t - piece.astype(_F32)
    return pieces


def _diffattn_body(par_ref, q_ref, k_ref, v_ref, lq1_ref, lk1_ref, lq2_ref, lk2_ref, g_ref, o_ref,
                   qm_ref, kf_ref, vt_ref, kn_ref, a0_ref, a1_ref, m_ref, acc_ref, *, tq, tk, seq):
    h = pl.program_id(1)
    qi = pl.program_id(2)
    nk = seq // tk
    slope = par_ref[h]
    inv_slope = par_ref[B_HEADS + h]
    lane_k = lax.broadcasted_iota(jnp.int32, (tk, HEAD_DIM), 1)

    def max_sq_norm(x):
        sq = x.astype(_F32) ** 2
        lane = lax.broadcasted_iota(jnp.int32, sq.shape, 1)
        s0 = jnp.sum(jnp.where(lane < B_QK_DIM, sq, 0.0), axis=1, keepdims=True)
        s1 = jnp.sum(jnp.where(lane >= B_QK_DIM, sq, 0.0), axis=1, keepdims=True)
        return jnp.max(jnp.maximum(s0, s1), axis=0, keepdims=True)

    @pl.when(qi == 0)
    def _():
        r = lax.broadcasted_iota(jnp.int32, (tk, HEAD_DIM), 0).astype(_F32)
        f = jnp.zeros((tk, HEAD_DIM), _F32)
        for c, piece in enumerate(_split_bf16(slope * r)):
            f = jnp.where(lane_k == c, piece.astype(_F32), f)
        kf_ref[...] = f.astype(_BF16)
        ones = jnp.ones((_ONES_ROWS, tk), _BF16)
        kn2 = jnp.zeros((1, 1), _F32)
        for t in range(nk):
            vt_ref[t] = jnp.concatenate([v_ref[t * tk:(t + 1) * tk, :].T, ones], axis=0)
            kn2 = jnp.maximum(kn2, max_sq_norm(k_ref[t * tk:(t + 1) * tk, :]))
        kn_ref[...] = kn2

    q = q_ref[...]
    lane = lax.broadcasted_iota(jnp.int32, (tq, HEAD_DIM), 1)
    zero = jnp.zeros_like(q)
    qmask = jnp.concatenate([jnp.where(lane < B_QK_DIM, q, zero), jnp.where(lane >= B_QK_DIM, q, zero)], axis=0)
    lane2 = lax.broadcasted_iota(jnp.int32, (2 * tq, HEAD_DIM), 1)
    gpos = jnp.where(lane2 < _N_SPLIT, 1.0, 0.0).astype(_BF16)
    qm_ref[0] = jnp.concatenate([qmask, gpos], axis=1)
    qm_ref[1] = jnp.concatenate([qmask, -gpos], axis=1)
    m_ref[...] = jnp.full_like(m_ref, -jnp.inf)
    acc_ref[...] = jnp.zeros_like(acc_ref)

    col = lax.broadcasted_iota(jnp.int32, (1, 2 * tq), 1)
    qpos = (qi * tq + jnp.where(col < tq, col, col - tq)).astype(_F32)

    def update(a, c, vt):
        m_old = m_ref[...]
        m_new = jnp.maximum(m_old, jnp.max(a, axis=0, keepdims=True) + c)
        alpha = jnp.exp(m_old - m_new)
        p = jnp.exp(a - (m_new - c))
        pv = jnp.dot(vt, p.astype(_BF16), preferred_element_type=_F32)
        acc_ref[...] = alpha * acc_ref[...] + pv
        m_ref[...] = m_new

    q0 = qi * tq
    kd = q0 // tk
    reach = (_EXP_ZERO + _NORM_SLACK * jnp.sqrt(max_sq_norm(q) * kn_ref[...])) * inv_slope
    reach = jnp.where(reach < seq, reach, float(seq)).astype(jnp.int32)[0, 0] + 1
    lo = jnp.clip((q0 - reach) // tk, 0, kd)
    hi = jnp.clip((q0 + tq - 1 + reach) // tk, kd, nk - 1)
    even = (hi - lo) % 2 == 0
    grow_hi = even & (hi < nk - 1)
    hi = hi + grow_hi.astype(jnp.int32)
    lo = lo - (even & jnp.logical_not(grow_hi)).astype(jnp.int32)
    n_far = hi - lo

    def far_index(t):
        ki = lo + t
        ki = ki + (ki >= kd).astype(jnp.int32)
        return ki, (ki > kd).astype(jnp.int32), pl.multiple_of(ki * tk, tk)

    def far_scores(t, a_ref):
        _, after, k0 = far_index(t)
        ka = jnp.concatenate([k_ref[pl.ds(k0, tk), :], kf_ref[...]], axis=1)
        a_ref[...] = lax.dot_general(ka, qm_ref[after], _NT, preferred_element_type=_F32)

    def far_fold(t, a_ref):
        ki, after, k0 = far_index(t)
        sign = (1 - 2 * after).astype(_F32)
        c = (sign * slope) * (k0.astype(_F32) - qpos)
        update(a_ref[...], c, vt_ref[ki])

    far_scores(jnp.int32(0), a0_ref)

    def far_pair(j, carry):
        t = 2 * j
        far_scores(t + 1, a1_ref)
        far_fold(t, a0_ref)
        far_scores(t + 2, a0_ref)
        far_fold(t + 1, a1_ref)
        return carry

    lax.fori_loop(0, (n_far - 1) // 2, far_pair, 0)
    far_fold(n_far - 1, a0_ref)

    k0 = pl.multiple_of(kd * tk, tk)
    st = lax.dot_general(k_ref[pl.ds(k0, tk), :], qm_ref[0, :, 0:HEAD_DIM], _NT, preferred_element_type=_F32)
    krow = lax.broadcasted_iota(jnp.int32, (tk, tq), 0)
    qcol = lax.broadcasted_iota(jnp.int32, (tk, tq), 1)
    bias = slope * jnp.abs(qcol - krow + (q0 - kd * tk)).astype(_F32)
    update(st - jnp.concatenate([bias, bias], axis=1), jnp.zeros((1, 2 * tq), _F32), vt_ref[kd])

    lam = (jnp.exp(jnp.sum(lq1_ref[...] * lk1_ref[...], axis=-1, keepdims=True))
           - jnp.exp(jnp.sum(lq2_ref[...] * lk2_ref[...], axis=-1, keepdims=True))
           + LAMBDA_INIT)
    acc = acc_ref[...]
    on = acc[0:B_V_DIM] / acc[B_V_DIM:B_V_DIM + 1]
    o = on[:, 0:tq] - lam * on[:, tq:2 * tq]
    y = o * lax.rsqrt(jnp.mean(o * o, axis=0, keepdims=True) + EPS)
    o_ref[...] = ((y.T * g_ref[...]) * (1.0 - LAMBDA_INIT)).astype(o_ref.dtype)


def _diffattn(proj, params_b, lq1, lk1, lq2, lk2, subln_g, *, batch, seq, tq=256, tk=512):
    t = batch * seq
    nq = seq // tq
    nk = seq // tk
    assert tk % tq == 0 and seq % tk == 0 and nk % 2 == 0 and nk >= 2
    small = lambda shape: pl.BlockSpec(shape, lambda b, h, i: (0, 0))
    return pl.pallas_call(
        functools.partial(_diffattn_body, tq=tq, tk=tk, seq=seq),
        grid=(batch, B_HEADS, nq),
        in_specs=[
            pl.BlockSpec(memory_space=pltpu.SMEM),
            pl.BlockSpec((tq, HEAD_DIM), lambda b, h, i: (b * nq + i, _BQ_BLK + h)),
            pl.BlockSpec((seq, HEAD_DIM), lambda b, h, i: (b, _BK_BLK + h)),
            pl.BlockSpec((seq, HEAD_DIM), lambda b, h, i: (b, _BV_BLK + h)),
            small((1, B_QK_DIM)), small((1, B_QK_DIM)), small((1, B_QK_DIM)), small((1, B_QK_DIM)),
            small((1, B_V_DIM)),
        ],
        out_specs=pl.BlockSpec((tq, B_V_DIM), lambda b, h, i: (b * nq + i, h)),
        out_shape=jax.ShapeDtypeStruct((t, B_V_W), _BF16),
        scratch_shapes=[
            pltpu.VMEM((2, 2 * tq, 2 * HEAD_DIM), _BF16),
            pltpu.VMEM((tk, HEAD_DIM), _BF16),
            pltpu.VMEM((nk, B_V_DIM + _ONES_ROWS, tk), _BF16),
            pltpu.VMEM((1, 1), _F32),
            pltpu.VMEM((tk, 2 * tq), _F32),
            pltpu.VMEM((tk, 2 * tq), _F32),
            pltpu.VMEM((1, 2 * tq), _F32),
            pltpu.VMEM((B_V_DIM + _ONES_ROWS, 2 * tq), _F32),
        ],
        compiler_params=pltpu.CompilerParams(
            dimension_semantics=("parallel", "parallel", "arbitrary"),
            vmem_limit_bytes=_VMEM_LIMIT_BYTES),
        name="diffattn",
    )(params_b, proj, proj, proj, lq1, lk1, lq2, lk2, subln_g)


def _outproj_body(oa_ref, ob_ref, w_ref, x_ref, o_ref):
    a = jnp.concatenate([oa_ref[...], ob_ref[...]], axis=1)
    o_ref[...] = x_ref[...] + jnp.dot(a, w_ref[...], preferred_element_type=_F32)


def _outproj(oa, ob, w, x2d, *, tm=512):
    t = x2d.shape[0]
    return pl.pallas_call(
        _outproj_body,
        grid=(t // tm,),
        in_specs=[
            pl.BlockSpec((tm, A_Q_W), lambda i: (i, 0)),
            pl.BlockSpec((tm, B_V_W), lambda i: (i, 0)),
            pl.BlockSpec((MIX_WIDTH, D_MODEL), lambda i: (0, 0)),
            pl.BlockSpec((tm, D_MODEL), lambda i: (i, 0)),
        ],
        out_specs=pl.BlockSpec((tm, D_MODEL), lambda i: (i, 0)),
        out_shape=jax.ShapeDtypeStruct((t, D_MODEL), _F32),
        compiler_params=pltpu.CompilerParams(
            dimension_semantics=("parallel",),
            vmem_limit_bytes=_VMEM_LIMIT_BYTES),
        name="outproj",
    )(oa, ob, w, x2d)


def _mlp_body(x_ref, gm_ref, wu_ref, wd_ref, gf_ref, o_ref, h_ref, acc_ref):
    j = pl.program_id(1)

    @pl.when(j == 0)
    def _():
        x = x_ref[...]
        h_ref[...] = (_rms_scale(x) * gm_ref[...]).astype(_BF16)
        acc_ref[...] = x

    u = jnp.maximum(jnp.dot(h_ref[...], wu_ref[...], preferred_element_type=_F32), 0.0)
    acc_ref[...] += jnp.dot((u * u).astype(_BF16), wd_ref[...], preferred_element_type=_F32)

    @pl.when(j == pl.num_programs(1) - 1)
    def _():
        o_ref[...] = _rms_scale(acc_ref[...]) * gf_ref[...]


def _mlp(x1, g_mlp, w_up, w_down, g_final, *, tm=512, tf=1024):
    t = x1.shape[0]
    return pl.pallas_call(
        _mlp_body,
        grid=(t // tm, D_FF // tf),
        in_specs=[
            pl.BlockSpec((tm, D_MODEL), lambda i, j: (i, 0)),
            pl.BlockSpec((1, D_MODEL), lambda i, j: (0, 0)),
            pl.BlockSpec((D_MODEL, tf), lambda i, j: (0, j)),
            pl.BlockSpec((tf, D_MODEL), lambda i, j: (j, 0)),
            pl.BlockSpec((1, D_MODEL), lambda i, j: (0, 0)),
        ],
        out_specs=pl.BlockSpec((tm, D_MODEL), lambda i, j: (i, 0)),
        out_shape=jax.ShapeDtypeStruct((t, D_MODEL), _F32),
        scratch_shapes=[pltpu.VMEM((tm, D_MODEL), _BF16), pltpu.VMEM((tm, D_MODEL), _F32)],
        compiler_params=pltpu.CompilerParams(
            dimension_semantics=("parallel", "arbitrary"),
            vmem_limit_bytes=_VMEM_LIMIT_BYTES),
        name="mlp",
    )(x1, g_mlp, w_up, w_down, g_final)


def _alibi_slopes():
    i = jnp.arange(1, N_HEADS_TOTAL + 1, dtype=_F32)
    s = jnp.exp2(-8.0 / N_HEADS_TOTAL * i)
    return s[0::2], s[1::2]


def _encoder(x, p):
    batch, seq = x.shape[0], x.shape[1]
    x2d = x.reshape(batch * seq, D_MODEL)
    proj = _inproj(x2d, p["g_attn"], p["w_in"], p["colscale"])
    oa = _winattn(proj, p["params_a"], batch=batch, seq=seq)
    ob = _diffattn(proj, p["params_b"], p["lq1"], p["lk1"], p["lq2"], p["lk2"], p["subln_g"],
                   batch=batch, seq=seq)
    x1 = _outproj(oa, ob, p["w_out"], x2d)
    y = _mlp(x1, p["g_mlp"], p["w_up"], p["w_down"], p["g_final"])
    return y.reshape(batch, seq, D_MODEL)


def kernel(x_prompt, x_sample, norm_attn_g, w_in, sink_logits, lambda_q1, lambda_k1, lambda_q2, lambda_k2,
           diff_subln_g, w_out, norm_mlp_g, w_up, w_down, norm_final_g):
    slopes_a, slopes_b = _alibi_slopes()
    col = jnp.arange(IN_WIDTH)
    in_bq = (col >= _BQ_BLK * HEAD_DIM) & (col < _BK_BLK * HEAD_DIM)
    p = {
        "g_attn": norm_attn_g[0].reshape(1, D_MODEL).astype(_F32),
        "w_in": w_in[0].astype(_BF16),
        "colscale": jnp.where(in_bq, 1.0 / math.sqrt(B_QK_DIM), 1.0).astype(_F32).reshape(1, IN_WIDTH),
        "params_a": jnp.concatenate([slopes_a, sink_logits[0].astype(_F32)]),
        "params_b": jnp.concatenate([slopes_b, 1.0 / slopes_b]),
        "lq1": lambda_q1[0].reshape(1, B_QK_DIM).astype(_F32),
        "lk1": lambda_k1[0].reshape(1, B_QK_DIM).astype(_F32),
        "lq2": lambda_q2[0].reshape(1, B_QK_DIM).astype(_F32),
        "lk2": lambda_k2[0].reshape(1, B_QK_DIM).astype(_F32),
        "subln_g": diff_subln_g[0].reshape(1, B_V_DIM).astype(_F32),
        "w_out": w_out[0].astype(_BF16),
        "g_mlp": norm_mlp_g[0].reshape(1, D_MODEL).astype(_F32),
        "w_up": w_up[0].astype(_BF16),
        "w_down": w_down[0].astype(_BF16),
        "g_final": norm_final_g.reshape(1, D_MODEL).astype(_F32),
    }
    return (_encoder(x_prompt, p), _encoder(x_sample, p))
```

```python
import functools
import math

import jax
import jax.numpy as jnp
from jax import lax
from jax.experimental import pallas as pl
from jax.experimental.pallas import tpu as pltpu

D_MODEL = 2048
HEAD_DIM = 128
N_HEADS_TOTAL = D_MODEL // HEAD_DIM
A_HEADS = N_HEADS_TOTAL // 2
A_KV_HEADS = 2
A_GROUP = A_HEADS // A_KV_HEADS
WINDOW = 128
BLOCK = 128
B_HEADS = N_HEADS_TOTAL - A_HEADS
B_QK_DIM = HEAD_DIM // 2
B_V_DIM = HEAD_DIM
MIX_WIDTH = A_HEADS * HEAD_DIM + B_HEADS * B_V_DIM
D_FF = 4 * D_MODEL
EPS = 1e-5
A_Q_W = A_HEADS * HEAD_DIM
A_KV_W = A_KV_HEADS * HEAD_DIM
B_QK_W = B_HEADS * 2 * B_QK_DIM
B_V_W = B_HEADS * B_V_DIM
IN_WIDTH = A_Q_W + 2 * A_KV_W + 2 * B_QK_W + B_V_W
LAMBDA_INIT = 0.8 - 0.6 * math.exp(-0.3 * 0)

_AK_BLK = A_Q_W // HEAD_DIM
_AV_BLK = (A_Q_W + A_KV_W) // HEAD_DIM
_BQ_BLK = (A_Q_W + 2 * A_KV_W) // HEAD_DIM
_BK_BLK = _BQ_BLK + B_QK_W // HEAD_DIM
_BV_BLK = _BK_BLK + B_QK_W // HEAD_DIM

_VMEM_LIMIT_BYTES = 56 * 1024 * 1024

_NT = (((1,), (1,)), ((), ()))
_TN = (((0,), (0,)), ((), ()))

_BF16 = jnp.bfloat16
_F32 = jnp.float32
_LOG2E = math.log2(math.e)


def _rms_scale(x):
    return x * lax.rsqrt(jnp.mean(x * x, axis=-1, keepdims=True) + EPS)


def _inproj_body(x_ref, g_ref, w_ref, cs_ref, o_ref, h_ref):
    @pl.when(pl.program_id(1) == 0)
    def _():
        h_ref[...] = (_rms_scale(x_ref[...]) * g_ref[...]).astype(_BF16)

    acc = jnp.dot(h_ref[...], w_ref[...], preferred_element_type=_F32)
    o_ref[...] = (acc * cs_ref[...]).astype(o_ref.dtype)


def _inproj(x2d, g, w, colscale, *, tm=1024, tn=1536):
    t = x2d.shape[0]
    return pl.pallas_call(
        _inproj_body,
        grid=(t // tm, IN_WIDTH // tn),
        in_specs=[
            pl.BlockSpec((tm, D_MODEL), lambda i, j: (i, 0)),
            pl.BlockSpec((1, D_MODEL), lambda i, j: (0, 0)),
            pl.BlockSpec((D_MODEL, tn), lambda i, j: (0, j)),
            pl.BlockSpec((1, tn), lambda i, j: (0, j)),
        ],
        out_specs=pl.BlockSpec((tm, tn), lambda i, j: (i, j)),
        out_shape=jax.ShapeDtypeStruct((t, IN_WIDTH), _BF16),
        scratch_shapes=[pltpu.VMEM((tm, D_MODEL), _BF16)],
        compiler_params=pltpu.CompilerParams(
            dimension_semantics=("parallel", "arbitrary"),
            vmem_limit_bytes=_VMEM_LIMIT_BYTES),
        name="inproj",
    )(x2d, g, w, colscale)


_WIN_QBLOCKS = 8


def _winattn_body(par_ref, q_ref, kp_ref, kc_ref, kn_ref, vp_ref, vc_ref, vn_ref, o_ref, *, seq):
    i = pl.program_id(1)
    kv = pl.program_id(2)
    kall = jnp.concatenate([kp_ref[...], kc_ref[...], kn_ref[...]], axis=0)
    vall = jnp.concatenate([vp_ref[...], vc_ref[...], vn_ref[...]], axis=0)
    qi = lax.broadcasted_iota(jnp.int32, (BLOCK, 3 * BLOCK), 0)
    kj = lax.broadcasted_iota(jnp.int32, (BLOCK, 3 * BLOCK), 1)
    dist = jnp.abs(kj - BLOCK - qi)
    distf = dist.astype(_F32)
    in_window = dist <= WINDOW
    krow = lax.broadcasted_iota(jnp.int32, (1, 3 * BLOCK), 1)
    bias2, sink2 = [], []
    for g in range(A_GROUP):
        slope = par_ref[kv * A_GROUP + g]
        bias2.append(jnp.where(in_window, (-_LOG2E * slope) * distf, -jnp.inf))
        sink2.append(_LOG2E * par_ref[A_HEADS + kv * A_GROUP + g])
    def scores(blk):
        rows = slice(blk * BLOCK, (blk + 1) * BLOCK)
        q = jnp.concatenate([q_ref[rows, g * HEAD_DIM:(g + 1) * HEAD_DIM] for g in range(A_GROUP)], axis=0)
        return lax.dot_general(q, kall[blk * BLOCK:(blk + 3) * BLOCK], _NT, preferred_element_type=_F32)

    def softmax(blk, s):
        kpos = (i * _WIN_QBLOCKS + blk - 1) * BLOCK + krow
        edge2 = jnp.where((kpos >= 0) & (kpos < seq), 0.0, -jnp.inf)
        ps = []
        for g in range(A_GROUP):
            t = s[g * BLOCK:(g + 1) * BLOCK] * (_LOG2E / math.sqrt(HEAD_DIM)) + bias2[g] + edge2
            m = jnp.maximum(jnp.max(t, axis=-1, keepdims=True), sink2[g])
            p = jnp.exp2(t - m)
            den = jnp.sum(p, axis=-1, keepdims=True) + jnp.exp2(sink2[g] - m)
            ps.append((p * (1.0 / den)).astype(_BF16))
        return jnp.concatenate(ps, axis=0)

    def values(blk, p):
        rows = slice(blk * BLOCK, (blk + 1) * BLOCK)
        o = jnp.dot(p, vall[blk * BLOCK:(blk + 3) * BLOCK], preferred_element_type=_F32)
        for g in range(A_GROUP):
            o_ref[rows, g * HEAD_DIM:(g + 1) * HEAD_DIM] = o[g * BLOCK:(g + 1) * BLOCK].astype(o_ref.dtype)

    s_cur = scores(0)
    p_prev = None
    for blk in range(_WIN_QBLOCKS):
        s_next = scores(blk + 1) if blk + 1 < _WIN_QBLOCKS else None
        p_cur = softmax(blk, s_cur)
        if p_prev is not None:
            values(blk - 1, p_prev)
        s_cur, p_prev = s_next, p_cur
    values(_WIN_QBLOCKS - 1, p_prev)


def _winattn(proj, params_a, *, batch, seq):
    t = batch * seq
    nb = seq // BLOCK
    qb = _WIN_QBLOCKS
    assert nb % qb == 0
    nq = nb // qb
    qw = A_GROUP * HEAD_DIM

    def edge_spec(col0, first):
        def imap(b, i, kv):
            n = jnp.clip(i * qb - 1, 0, nb - 1) if first else jnp.clip((i + 1) * qb, 0, nb - 1)
            return (b * nb + n, col0 + kv)
        return pl.BlockSpec((BLOCK, HEAD_DIM), imap)

    def mid_spec(col0):
        return pl.BlockSpec((qb * BLOCK, HEAD_DIM), lambda b, i, kv: (b * nq + i, col0 + kv))

    return pl.pallas_call(
        functools.partial(_winattn_body, seq=seq),
        grid=(batch, nq, A_KV_HEADS),
        in_specs=[
            pl.BlockSpec(memory_space=pltpu.SMEM),
            pl.BlockSpec((qb * BLOCK, qw), lambda b, i, kv: (b * nq + i, kv)),
            edge_spec(_AK_BLK, True), mid_spec(_AK_BLK), edge_spec(_AK_BLK, False),
            edge_spec(_AV_BLK, True), mid_spec(_AV_BLK), edge_spec(_AV_BLK, False),
        ],
        out_specs=pl.BlockSpec((qb * BLOCK, qw), lambda b, i, kv: (b * nq + i, kv)),
        out_shape=jax.ShapeDtypeStruct((t, A_Q_W), _BF16),
        compiler_params=pltpu.CompilerParams(
            dimension_semantics=("parallel", "parallel", "parallel")),
        name="winattn",
    )(params_a, proj, proj, proj, proj, proj, proj, proj)


_ONES_ROWS = 16
_N_SPLIT = 3
_EXP_ZERO = 110.0
_NORM_SLACK = 2.1


def _split_bf16(t):
    pieces = []
    for _ in range(_N_SPLIT):
        piece = t.astype(_BF16)
        pieces.append(piece)
        t = t - piece.astype(_F32)
    return pieces


def _diffattn_body(par_ref, q_ref, k_ref, v_ref, lq1_ref, lk1_ref, lq2_ref, lk2_ref, g_ref, o_ref,
                   qm_ref, kf_ref, vt_ref, kn_ref, a0_ref, a1_ref, m_ref, acc_ref, *, tq, tk, seq):
    h = pl.program_id(1)
    qi = pl.program_id(2)
    nk = seq // tk
    slope = par_ref[h]
    inv_slope = par_ref[B_HEADS + h]
    lane_k = lax.broadcasted_iota(jnp.int32, (tk, HEAD_DIM), 1)

    def max_sq_norm(x):
        sq = x.astype(_F32) ** 2
        lane = lax.broadcasted_iota(jnp.int32, sq.shape, 1)
        s0 = jnp.sum(jnp.where(lane < B_QK_DIM, sq, 0.0), axis=1, keepdims=True)
        s1 = jnp.sum(jnp.where(lane >= B_QK_DIM, sq, 0.0), axis=1, keepdims=True)
        return jnp.max(jnp.maximum(s0, s1), axis=0, keepdims=True)

    @pl.when(qi == 0)
    def _():
        r = lax.broadcasted_iota(jnp.int32, (tk, HEAD_DIM), 0).astype(_F32)
        f = jnp.zeros((tk, HEAD_DIM), _F32)
        for c, piece in enumerate(_split_bf16(slope * r)):
            f = jnp.where(lane_k == c, piece.astype(_F32), f)
        kf_ref[...] = f.astype(_BF16)
        ones = jnp.ones((_ONES_ROWS, tk), _BF16)
        kn2 = jnp.zeros((1, 1), _F32)
        for t in range(nk):
            vt_ref[t] = jnp.concatenate([v_ref[t * tk:(t + 1) * tk, :].T, ones], axis=0)
            kn2 = jnp.maximum(kn2, max_sq_norm(k_ref[t * tk:(t + 1) * tk, :]))
        kn_ref[...] = kn2

    q = q_ref[...]
    lane = lax.broadcasted_iota(jnp.int32, (tq, HEAD_DIM), 1)
    zero = jnp.zeros_like(q)
    qmask = jnp.concatenate([jnp.where(lane < B_QK_DIM, q, zero), jnp.where(lane >= B_QK_DIM, q, zero)], axis=0)
    lane2 = lax.broadcasted_iota(jnp.int32, (2 * tq, HEAD_DIM), 1)
    gpos = jnp.where(lane2 < _N_SPLIT, 1.0, 0.0).astype(_BF16)
    qm_ref[0] = jnp.concatenate([qmask, gpos], axis=1)
    qm_ref[1] = jnp.concatenate([qmask, -gpos], axis=1)
    m_ref[...] = jnp.full_like(m_ref, -jnp.inf)
    acc_ref[...] = jnp.zeros_like(acc_ref)

    col = lax.broadcasted_iota(jnp.int32, (1, 2 * tq), 1)
    qpos = (qi * tq + jnp.where(col < tq, col, col - tq)).astype(_F32)

    def update(a, c, vt):
        m_old = m_ref[...]
        m_new = jnp.maximum(m_old, jnp.max(a, axis=0, keepdims=True) + c)
        alpha = jnp.exp(m_old - m_new)
        p = jnp.exp(a - (m_new - c))
        pv = jnp.dot(vt, p.astype(_BF16), preferred_element_type=_F32)
        acc_ref[...] = alpha * acc_ref[...] + pv
        m_ref[...] = m_new

    q0 = qi * tq
    kd = q0 // tk
    reach = (_EXP_ZERO + _NORM_SLACK * jnp.sqrt(max_sq_norm(q) * kn_ref[...])) * inv_slope
    reach = jnp.where(reach < seq, reach, float(seq)).astype(jnp.int32)[0, 0] + 1
    lo = jnp.clip((q0 - reach) // tk, 0, kd)
    hi = jnp.clip((q0 + tq - 1 + reach) // tk, kd, nk - 1)
    even = (hi - lo) % 2 == 0
    grow_hi = even & (hi < nk - 1)
    hi = hi + grow_hi.astype(jnp.int32)
    lo = lo - (even & jnp.logical_not(grow_hi)).astype(jnp.int32)
    n_far = hi - lo

    def far_index(t):
        ki = lo + t
        ki = ki + (ki >= kd).astype(jnp.int32)
        return ki, (ki > kd).astype(jnp.int32), pl.multiple_of(ki * tk, tk)

    def far_scores(t, a_ref):
        _, after, k0 = far_index(t)
        ka = jnp.concatenate([k_ref[pl.ds(k0, tk), :], kf_ref[...]], axis=1)
        a_ref[...] = lax.dot_general(ka, qm_ref[after], _NT, preferred_element_type=_F32)

    def far_fold(t, a_ref):
        ki, after, k0 = far_index(t)
        sign = (1 - 2 * after).astype(_F32)
        c = (sign * slope) * (k0.astype(_F32) - qpos)
        update(a_ref[...], c, vt_ref[ki])

    def near_scores(a_ref):
        k0 = pl.multiple_of(kd * tk, tk)
        a_ref[...] = lax.dot_general(k_ref[pl.ds(k0, tk), :], qm_ref[0, :, 0:HEAD_DIM], _NT,
                                     preferred_element_type=_F32)

    def near_fold(a_ref):
        krow = lax.broadcasted_iota(jnp.int32, (tk, tq), 0)
        qcol = lax.broadcasted_iota(jnp.int32, (tk, tq), 1)
        bias = slope * jnp.abs(qcol - krow + (q0 - kd * tk)).astype(_F32)
        update(a_ref[...] - jnp.concatenate([bias, bias], axis=1), jnp.zeros((1, 2 * tq), _F32), vt_ref[kd])

    near_scores(a0_ref)
    far_scores(jnp.int32(0), a1_ref)
    near_fold(a0_ref)

    def far_pair(j, carry):
        t = 2 * j
        far_scores(t + 1, a0_ref)
        far_fold(t, a1_ref)
        far_scores(t + 2, a1_ref)
        far_fold(t + 1, a0_ref)
        return carry

    lax.fori_loop(0, (n_far - 1) // 2, far_pair, 0)
    far_fold(n_far - 1, a1_ref)

    lam = (jnp.exp(jnp.sum(lq1_ref[...] * lk1_ref[...], axis=-1, keepdims=True))
           - jnp.exp(jnp.sum(lq2_ref[...] * lk2_ref[...], axis=-1, keepdims=True))
           + LAMBDA_INIT)
    acc = acc_ref[...]
    on = acc[0:B_V_DIM] / acc[B_V_DIM:B_V_DIM + 1]
    o = on[:, 0:tq] - lam * on[:, tq:2 * tq]
    y = o * lax.rsqrt(jnp.mean(o * o, axis=0, keepdims=True) + EPS)
    o_ref[...] = ((y.T * g_ref[...]) * (1.0 - LAMBDA_INIT)).astype(o_ref.dtype)


def _diffattn(proj, params_b, lq1, lk1, lq2, lk2, subln_g, *, batch, seq, tq=256, tk=512):
    t = batch * seq
    nq = seq // tq
    nk = seq // tk
    assert tk % tq == 0 and seq % tk == 0 and nk % 2 == 0 and nk >= 2
    small = lambda shape: pl.BlockSpec(shape, lambda b, h, i: (0, 0))
    return pl.pallas_call(
        functools.partial(_diffattn_body, tq=tq, tk=tk, seq=seq),
        grid=(batch, B_HEADS, nq),
        in_specs=[
            pl.BlockSpec(memory_space=pltpu.SMEM),
            pl.BlockSpec((tq, HEAD_DIM), lambda b, h, i: (b * nq + i, _BQ_BLK + h)),
            pl.BlockSpec((seq, HEAD_DIM), lambda b, h, i: (b, _BK_BLK + h)),
            pl.BlockSpec((seq, HEAD_DIM), lambda b, h, i: (b, _BV_BLK + h)),
            small((1, B_QK_DIM)), small((1, B_QK_DIM)), small((1, B_QK_DIM)), small((1, B_QK_DIM)),
            small((1, B_V_DIM)),
        ],
        out_specs=pl.BlockSpec((tq, B_V_DIM), lambda b, h, i: (b * nq + i, h)),
        out_shape=jax.ShapeDtypeStruct((t, B_V_W), _BF16),
        scratch_shapes=[
            pltpu.VMEM((2, 2 * tq, 2 * HEAD_DIM), _BF16),
            pltpu.VMEM((tk, HEAD_DIM), _BF16),
            pltpu.VMEM((nk, B_V_DIM + _ONES_ROWS, tk), _BF16),
            pltpu.VMEM((1, 1), _F32),
            pltpu.VMEM((tk, 2 * tq), _F32),
            pltpu.VMEM((tk, 2 * tq), _F32),
            pltpu.VMEM((1, 2 * tq), _F32),
            pltpu.VMEM((B_V_DIM + _ONES_ROWS, 2 * tq), _F32),
        ],
        compiler_params=pltpu.CompilerParams(
            dimension_semantics=("parallel", "parallel", "arbitrary"),
            vmem_limit_bytes=_VMEM_LIMIT_BYTES),
        name="diffattn",
    )(params_b, proj, proj, proj, lq1, lk1, lq2, lk2, subln_g)


def _outproj_body(oa_ref, ob_ref, w_ref, x_ref, o_ref):
    a = jnp.concatenate([oa_ref[...], ob_ref[...]], axis=1)
    o_ref[...] = x_ref[...] + jnp.dot(a, w_ref[...], preferred_element_type=_F32)


def _outproj(oa, ob, w, x2d, *, tm=512):
    t = x2d.shape[0]
    return pl.pallas_call(
        _outproj_body,
        grid=(t // tm,),
        in_specs=[
            pl.BlockSpec((tm, A_Q_W), lambda i: (i, 0)),
            pl.BlockSpec((tm, B_V_W), lambda i: (i, 0)),
            pl.BlockSpec((MIX_WIDTH, D_MODEL), lambda i: (0, 0)),
            pl.BlockSpec((tm, D_MODEL), lambda i: (i, 0)),
        ],
        out_specs=pl.BlockSpec((tm, D_MODEL), lambda i: (i, 0)),
        out_shape=jax.ShapeDtypeStruct((t, D_MODEL), _F32),
        compiler_params=pltpu.CompilerParams(
            dimension_semantics=("parallel",),
            vmem_limit_bytes=_VMEM_LIMIT_BYTES),
        name="outproj",
    )(oa, ob, w, x2d)


def _mlp_body(x_ref, gm_ref, wu_ref, wd_ref, gf_ref, o_ref, h_ref, acc_ref):
    j = pl.program_id(1)

    @pl.when(j == 0)
    def _():
        x = x_ref[...]
        h_ref[...] = (_rms_scale(x) * gm_ref[...]).astype(_BF16)
        acc_ref[...] = x

    u = jnp.maximum(jnp.dot(h_ref[...], wu_ref[...], preferred_element_type=_F32), 0.0)
    acc_ref[...] += jnp.dot((u * u).astype(_BF16), wd_ref[...], preferred_element_type=_F32)

    @pl.when(j == pl.num_programs(1) - 1)
    def _():
        o_ref[...] = _rms_scale(acc_ref[...]) * gf_ref[...]


def _mlp(x1, g_mlp, w_up, w_down, g_final, *, tm=512, tf=1024):
    t = x1.shape[0]
    return pl.pallas_call(
        _mlp_body,
        grid=(t // tm, D_FF // tf),
        in_specs=[
            pl.BlockSpec((tm, D_MODEL), lambda i, j: (i, 0)),
            pl.BlockSpec((1, D_MODEL), lambda i, j: (0, 0)),
            pl.BlockSpec((D_MODEL, tf), lambda i, j: (0, j)),
            pl.BlockSpec((tf, D_MODEL), lambda i, j: (j, 0)),
            pl.BlockSpec((1, D_MODEL), lambda i, j: (0, 0)),
        ],
        out_specs=pl.BlockSpec((tm, D_MODEL), lambda i, j: (i, 0)),
        out_shape=jax.ShapeDtypeStruct((t, D_MODEL), _F32),
        scratch_shapes=[pltpu.VMEM((tm, D_MODEL), _BF16), pltpu.VMEM((tm, D_MODEL), _F32)],
        compiler_params=pltpu.CompilerParams(
            dimension_semantics=("parallel", "arbitrary"),
            vmem_limit_bytes=_VMEM_LIMIT_BYTES),
        name="mlp",
    )(x1, g_mlp, w_up, w_down, g_final)


def _alibi_slopes():
    i = jnp.arange(1, N_HEADS_TOTAL + 1, dtype=_F32)
    s = jnp.exp2(-8.0 / N_HEADS_TOTAL * i)
    return s[0::2], s[1::2]


def _encoder(x, p):
    batch, seq = x.shape[0], x.shape[1]
    x2d = x.reshape(batch * seq, D_MODEL)
    proj = _inproj(x2d, p["g_attn"], p["w_in"], p["colscale"])
    oa = _winattn(proj, p["params_a"], batch=batch, seq=seq)
    ob = _diffattn(proj, p["params_b"], p["lq1"], p["lk1"], p["lq2"], p["lk2"], p["subln_g"],
                   batch=batch, seq=seq)
    x1 = _outproj(oa, ob, p["w_out"], x2d)
    y = _mlp(x1, p["g_mlp"], p["w_up"], p["w_down"], p["g_final"])
    return y.reshape(batch, seq, D_MODEL)


def kernel(x_prompt, x_sample, norm_attn_g, w_in, sink_logits, lambda_q1, lambda_k1, lambda_q2, lambda_k2,
           diff_subln_g, w_out, norm_mlp_g, w_up, w_down, norm_final_g):
    slopes_a, slopes_b = _alibi_slopes()
    col = jnp.arange(IN_WIDTH)
    in_bq = (col >= _BQ_BLK * HEAD_DIM) & (col < _BK_BLK * HEAD_DIM)
    p = {
        "g_attn": norm_attn_g[0].reshape(1, D_MODEL).astype(_F32),
        "w_in": w_in[0].astype(_BF16),
        "colscale": jnp.where(in_bq, 1.0 / math.sqrt(B_QK_DIM), 1.0).astype(_F32).reshape(1, IN_WIDTH),
        "params_a": jnp.concatenate([slopes_a, sink_logits[0].astype(_F32)]),
        "params_b": jnp.concatenate([slopes_b, 1.0 / slopes_b]),
        "lq1": lambda_q1[0].reshape(1, B_QK_DIM).astype(_F32),
        "lk1": lambda_k1[0].reshape(1, B_QK_DIM).astype(_F32),
        "lq2": lambda_q2[0].reshape(1, B_QK_DIM).astype(_F32),
        "lk2": lambda_k2[0].reshape(1, B_QK_DIM).astype(_F32),
        "subln_g": diff_subln_g[0].reshape(1, B_V_DIM).astype(_F32),
        "w_out": w_out[0].astype(_BF16),
        "g_mlp": norm_mlp_g[0].reshape(1, D_MODEL).astype(_F32),
        "w_up": w_up[0].astype(_BF16),
        "w_down": w_down[0].astype(_BF16),
        "g_final": norm_final_g.reshape(1, D_MODEL).astype(_F32),
    }
    return (_encoder(x_prompt, p), _encoder(x_sample, p))
```

```python
import functools
import math

import jax
import jax.numpy as jnp
from jax import lax
from jax.experimental import pallas as pl
from jax.experimental.pallas import tpu as pltpu

D_MODEL = 2048
HEAD_DIM = 128
N_HEADS_TOTAL = D_MODEL // HEAD_DIM
A_HEADS = N_HEADS_TOTAL // 2
A_KV_HEADS = 2
A_GROUP = A_HEADS // A_KV_HEADS
WINDOW = 128
BLOCK = 128
B_HEADS = N_HEADS_TOTAL - A_HEADS
B_QK_DIM = HEAD_DIM // 2
B_V_DIM = HEAD_DIM
MIX_WIDTH = A_HEADS * HEAD_DIM + B_HEADS * B_V_DIM
D_FF = 4 * D_MODEL
EPS = 1e-5
A_Q_W = A_HEADS * HEAD_DIM
A_KV_W = A_KV_HEADS * HEAD_DIM
B_QK_W = B_HEADS * 2 * B_QK_DIM
B_V_W = B_HEADS * B_V_DIM
IN_WIDTH = A_Q_W + 2 * A_KV_W + 2 * B_QK_W + B_V_W
LAMBDA_INIT = 0.8 - 0.6 * math.exp(-0.3 * 0)

_AK_BLK = A_Q_W // HEAD_DIM
_AV_BLK = (A_Q_W + A_KV_W) // HEAD_DIM
_BQ_BLK = (A_Q_W + 2 * A_KV_W) // HEAD_DIM
_BK_BLK = _BQ_BLK + B_QK_W // HEAD_DIM
_BV_BLK = _BK_BLK + B_QK_W // HEAD_DIM

_VMEM_LIMIT_BYTES = 56 * 1024 * 1024

_NT = (((1,), (1,)), ((), ()))
_TN = (((0,), (0,)), ((), ()))

_BF16 = jnp.bfloat16
_F32 = jnp.float32
_LOG2E = math.log2(math.e)


def _rms_scale(x):
    return x * lax.rsqrt(jnp.mean(x * x, axis=-1, keepdims=True) + EPS)


def _inproj_body(x_ref, g_ref, w_ref, cs_ref, o_ref, h_ref):
    @pl.when(pl.program_id(1) == 0)
    def _():
        h_ref[...] = (_rms_scale(x_ref[...]) * g_ref[...]).astype(_BF16)

    acc = jnp.dot(h_ref[...], w_ref[...], preferred_element_type=_F32)
    o_ref[...] = (acc * cs_ref[...]).astype(o_ref.dtype)


def _inproj(x2d, g, w, colscale, *, tm=1024, tn=1536):
    t = x2d.shape[0]
    return pl.pallas_call(
        _inproj_body,
        grid=(t // tm, IN_WIDTH // tn),
        in_specs=[
            pl.BlockSpec((tm, D_MODEL), lambda i, j: (i, 0)),
            pl.BlockSpec((1, D_MODEL), lambda i, j: (0, 0)),
            pl.BlockSpec((D_MODEL, tn), lambda i, j: (0, j)),
            pl.BlockSpec((1, tn), lambda i, j: (0, j)),
        ],
        out_specs=pl.BlockSpec((tm, tn), lambda i, j: (i, j)),
        out_shape=jax.ShapeDtypeStruct((t, IN_WIDTH), _BF16),
        scratch_shapes=[pltpu.VMEM((tm, D_MODEL), _BF16)],
        compiler_params=pltpu.CompilerParams(
            dimension_semantics=("parallel", "arbitrary"),
            vmem_limit_bytes=_VMEM_LIMIT_BYTES),
        name="inproj",
    )(x2d, g, w, colscale)


_WIN_QBLOCKS = 8


def _winattn_body(par_ref, q_ref, kp_ref, kc_ref, kn_ref, vp_ref, vc_ref, vn_ref, o_ref, *, seq):
    i = pl.program_id(1)
    kv = pl.program_id(2)
    kall = jnp.concatenate([kp_ref[...], kc_ref[...], kn_ref[...]], axis=0)
    vall = jnp.concatenate([vp_ref[...], vc_ref[...], vn_ref[...]], axis=0)
    qi = lax.broadcasted_iota(jnp.int32, (BLOCK, 3 * BLOCK), 0)
    kj = lax.broadcasted_iota(jnp.int32, (BLOCK, 3 * BLOCK), 1)
    dist = jnp.abs(kj - BLOCK - qi)
    distf = dist.astype(_F32)
    in_window = dist <= WINDOW
    krow = lax.broadcasted_iota(jnp.int32, (1, 3 * BLOCK), 1)
    bias2, sink2 = [], []
    for g in range(A_GROUP):
        slope = par_ref[kv * A_GROUP + g]
        bias2.append(jnp.where(in_window, (-_LOG2E * slope) * distf, -jnp.inf))
        sink2.append(_LOG2E * par_ref[A_HEADS + kv * A_GROUP + g])
    def scores(blk):
        rows = slice(blk * BLOCK, (blk + 1) * BLOCK)
        q = jnp.concatenate([q_ref[rows, g * HEAD_DIM:(g + 1) * HEAD_DIM] for g in range(A_GROUP)], axis=0)
        return lax.dot_general(q, kall[blk * BLOCK:(blk + 3) * BLOCK], _NT, preferred_element_type=_F32)

    def softmax(blk, s):
        kpos = (i * _WIN_QBLOCKS + blk - 1) * BLOCK + krow
        edge2 = jnp.where((kpos >= 0) & (kpos < seq), 0.0, -jnp.inf)
        ps = []
        for g in range(A_GROUP):
            t = s[g * BLOCK:(g + 1) * BLOCK] * (_LOG2E / math.sqrt(HEAD_DIM)) + bias2[g] + edge2
            m = jnp.maximum(jnp.max(t, axis=-1, keepdims=True), sink2[g])
            p = jnp.exp2(t - m)
            den = jnp.sum(p, axis=-1, keepdims=True) + jnp.exp2(sink2[g] - m)
            ps.append((p * (1.0 / den)).astype(_BF16))
        return jnp.concatenate(ps, axis=0)

    def values(blk, p):
        rows = slice(blk * BLOCK, (blk + 1) * BLOCK)
        o = jnp.dot(p, vall[blk * BLOCK:(blk + 3) * BLOCK], preferred_element_type=_F32)
        for g in range(A_GROUP):
            o_ref[rows, g * HEAD_DIM:(g + 1) * HEAD_DIM] = o[g * BLOCK:(g + 1) * BLOCK].astype(o_ref.dtype)

    s_cur = scores(0)
    p_prev = None
    for blk in range(_WIN_QBLOCKS):
        s_next = scores(blk + 1) if blk + 1 < _WIN_QBLOCKS else None
        p_cur = softmax(blk, s_cur)
        if p_prev is not None:
            values(blk - 1, p_prev)
        s_cur, p_prev = s_next, p_cur
    values(_WIN_QBLOCKS - 1, p_prev)


def _winattn(proj, params_a, *, batch, seq):
    t = batch * seq
    nb = seq // BLOCK
    qb = _WIN_QBLOCKS
    assert nb % qb == 0
    nq = nb // qb
    qw = A_GROUP * HEAD_DIM

    def edge_spec(col0, first):
        def imap(b, i, kv):
            n = jnp.clip(i * qb - 1, 0, nb - 1) if first else jnp.clip((i + 1) * qb, 0, nb - 1)
            return (b * nb + n, col0 + kv)
        return pl.BlockSpec((BLOCK, HEAD_DIM), imap)

    def mid_spec(col0):
        return pl.BlockSpec((qb * BLOCK, HEAD_DIM), lambda b, i, kv: (b * nq + i, col0 + kv))

    return pl.pallas_call(
        functools.partial(_winattn_body, seq=seq),
        grid=(batch, nq, A_KV_HEADS),
        in_specs=[
            pl.BlockSpec(memory_space=pltpu.SMEM),
            pl.BlockSpec((qb * BLOCK, qw), lambda b, i, kv: (b * nq + i, kv)),
            edge_spec(_AK_BLK, True), mid_spec(_AK_BLK), edge_spec(_AK_BLK, False),
            edge_spec(_AV_BLK, True), mid_spec(_AV_BLK), edge_spec(_AV_BLK, False),
        ],
        out_specs=pl.BlockSpec((qb * BLOCK, qw), lambda b, i, kv: (b * nq + i, kv)),
        out_shape=jax.ShapeDtypeStruct((t, A_Q_W), _BF16),
        compiler_params=pltpu.CompilerParams(
            dimension_semantics=("parallel", "parallel", "parallel")),
        name="winattn",
    )(params_a, proj, proj, proj, proj, proj, proj, proj)


_ONES_ROWS = 16
_N_SPLIT = 3
_EXP_ZERO = 110.0
_NORM_SLACK = 2.1


def _split_bf16(t):
    pieces = []
    for _ in range(_N_SPLIT):
        piece = t.astype(_BF16)
        pieces.append(piece)
        t = t - piece.astype(_F32)
    return pieces


def _diffattn_body(par_ref, q_ref, k_ref, v_ref, lq1_ref, lk1_ref, lq2_ref, lk2_ref, g_ref, o_ref,
                   qm_ref, ka_ref, vt_ref, kn_ref, a0_ref, a1_ref, m_ref, acc_ref, *, tq, tk, seq):
    h = pl.program_id(1)
    qi = pl.program_id(2)
    nk = seq // tk
    slope = par_ref[h]
    inv_slope = par_ref[B_HEADS + h]
    lane_k = lax.broadcasted_iota(jnp.int32, (tk, HEAD_DIM), 1)

    def half_sq_norms_max(n2):
        return jnp.max(jnp.max(n2, axis=0, keepdims=True), axis=1, keepdims=True)

    def max_sq_norm(x):
        sq = x.astype(_F32) ** 2
        lane = lax.broadcasted_iota(jnp.int32, sq.shape, 1)
        s0 = jnp.sum(jnp.where(lane < B_QK_DIM, sq, 0.0), axis=1, keepdims=True)
        s1 = jnp.sum(jnp.where(lane >= B_QK_DIM, sq, 0.0), axis=1, keepdims=True)
        return half_sq_norms_max(jnp.maximum(s0, s1))

    def max_sq_norm_mxu(x):
        sq = (x.astype(_F32) ** 2).astype(_BF16)
        row = lax.broadcasted_iota(jnp.int32, (HEAD_DIM, HEAD_DIM), 0)
        colm = lax.broadcasted_iota(jnp.int32, (HEAD_DIM, HEAD_DIM), 1)
        half_sum = jnp.where((row // B_QK_DIM) == colm, 1.0, 0.0).astype(_BF16)
        return half_sq_norms_max(jnp.dot(sq, half_sum, preferred_element_type=_F32))

    @pl.when(qi == 0)
    def _():
        r = lax.broadcasted_iota(jnp.int32, (tk, HEAD_DIM), 0).astype(_F32)
        feats = []
        for mp in range(2):
            base = (1 - mp) * B_QK_DIM
            f = jnp.zeros((tk, HEAD_DIM), _F32)
            for c, piece in enumerate(_split_bf16(slope * r)):
                f = jnp.where(lane_k == base + c, piece.astype(_F32), f)
            feats.append(f)
        ones = jnp.ones((_ONES_ROWS, tk), _BF16)
        kn2 = jnp.zeros((1, 1), _F32)
        for t in range(nk):
            vt_ref[t] = jnp.concatenate([v_ref[t * tk:(t + 1) * tk, :].T, ones], axis=0)
            k = k_ref[t * tk:(t + 1) * tk, :]
            kn2 = jnp.maximum(kn2, max_sq_norm_mxu(k))
            ka_ref[0, t * tk:(t + 1) * tk, :] = jnp.where(lane_k < B_QK_DIM, k.astype(_F32), feats[0]).astype(_BF16)
            ka_ref[1, t * tk:(t + 1) * tk, :] = jnp.where(lane_k >= B_QK_DIM, k.astype(_F32), feats[1]).astype(_BF16)
        kn_ref[...] = kn2

    q = q_ref[...]
    lane = lax.broadcasted_iota(jnp.int32, (tq, HEAD_DIM), 1)
    first_half = lane < B_QK_DIM
    for mp in range(2):
        own = first_half if mp == 0 else jnp.logical_not(first_half)
        base = (1 - mp) * B_QK_DIM
        sel = jnp.where((lane >= base) & (lane < base + _N_SPLIT), 1.0, 0.0)
        for kind, cols in enumerate((sel, -sel, jnp.zeros_like(sel))):
            qm_ref[kind, mp] = jnp.where(own, q.astype(_F32), cols).astype(_BF16)
    m_ref[...] = jnp.full_like(m_ref, -jnp.inf)
    acc_ref[...] = jnp.zeros_like(acc_ref)

    col = lax.broadcasted_iota(jnp.int32, (1, 2 * tq), 1)
    qpos = (qi * tq + jnp.where(col < tq, col, col - tq)).astype(_F32)

    def update(a, c, vt):
        m_old = m_ref[...]
        m_new = jnp.maximum(m_old, jnp.max(a, axis=0, keepdims=True) + c)
        alpha = jnp.exp(m_old - m_new)
        p = jnp.exp(a - (m_new - c))
        pv = jnp.dot(vt, p.astype(_BF16), preferred_element_type=_F32)
        acc_ref[...] = alpha * acc_ref[...] + pv
        m_ref[...] = m_new

    q0 = qi * tq
    kd = q0 // tk
    reach = (_EXP_ZERO + _NORM_SLACK * jnp.sqrt(max_sq_norm(q) * kn_ref[...])) * inv_slope
    reach = jnp.where(reach < seq, reach, float(seq)).astype(jnp.int32)[0, 0] + 1
    lo = jnp.clip((q0 - reach) // tk, 0, kd)
    hi = jnp.clip((q0 + tq - 1 + reach) // tk, kd, nk - 1)
    even = (hi - lo) % 2 == 0
    grow_hi = even & (hi < nk - 1)
    hi = hi + grow_hi.astype(jnp.int32)
    lo = lo - (even & jnp.logical_not(grow_hi)).astype(jnp.int32)
    n_far = hi - lo

    def far_index(t):
        ki = lo + t
        ki = ki + (ki >= kd).astype(jnp.int32)
        return ki, (ki > kd).astype(jnp.int32), pl.multiple_of(ki * tk, tk)

    def scores(k0, kind, a_ref):
        for mp in range(2):
            a_ref[:, mp * tq:(mp + 1) * tq] = lax.dot_general(
                ka_ref[mp, pl.ds(k0, tk), :], qm_ref[kind, mp], _NT, preferred_element_type=_F32)

    def far_scores(t, a_ref):
        _, after, k0 = far_index(t)
        scores(k0, after, a_ref)

    def far_fold(t, a_ref):
        ki, after, k0 = far_index(t)
        sign = (1 - 2 * after).astype(_F32)
        c = (sign * slope) * (k0.astype(_F32) - qpos)
        update(a_ref[...], c, vt_ref[ki])

    def near_scores(a_ref):
        scores(pl.multiple_of(kd * tk, tk), 2, a_ref)

    def near_fold(a_ref):
        krow = lax.broadcasted_iota(jnp.int32, (tk, tq), 0)
        qcol = lax.broadcasted_iota(jnp.int32, (tk, tq), 1)
        bias = slope * jnp.abs(qcol - krow + (q0 - kd * tk)).astype(_F32)
        update(a_ref[...] - jnp.concatenate([bias, bias], axis=1), jnp.zeros((1, 2 * tq), _F32), vt_ref[kd])

    near_scores(a0_ref)
    far_scores(jnp.int32(0), a1_ref)
    near_fold(a0_ref)

    def far_pair(j, carry):
        t = 2 * j
        far_scores(t + 1, a0_ref)
        far_fold(t, a1_ref)
        far_scores(t + 2, a1_ref)
        far_fold(t + 1, a0_ref)
        return carry

    def far_pairs(count):
        def body(j, carry):
            for u in range(count):
                carry = far_pair(count * j + u, carry)
            return carry
        return body

    n_pairs = (n_far - 1) // 2
    done = jnp.int32(0)
    for count in (4, 2, 1):
        trips = (n_pairs - done) // count
        lax.fori_loop(done // count, done // count + trips, far_pairs(count), 0)
        done = done + trips * count
    far_fold(n_far - 1, a1_ref)

    lam = (jnp.exp(jnp.sum(lq1_ref[...] * lk1_ref[...], axis=-1, keepdims=True))
           - jnp.exp(jnp.sum(lq2_ref[...] * lk2_ref[...], axis=-1, keepdims=True))
           + LAMBDA_INIT)
    acc = acc_ref[...]
    on = acc[0:B_V_DIM] / acc[B_V_DIM:B_V_DIM + 1]
    o = on[:, 0:tq] - lam * on[:, tq:2 * tq]
    y = o * lax.rsqrt(jnp.mean(o * o, axis=0, keepdims=True) + EPS)
    o_ref[...] = ((y.T * g_ref[...]) * (1.0 - LAMBDA_INIT)).astype(o_ref.dtype)


def _diffattn(proj, params_b, lq1, lk1, lq2, lk2, subln_g, *, batch, seq, tq=256, tk=512):
    t = batch * seq
    nq = seq // tq
    nk = seq // tk
    assert tk % tq == 0 and seq % tk == 0 and nk % 2 == 0 and nk >= 2
    small = lambda shape: pl.BlockSpec(shape, lambda b, h, i: (0, 0))
    return pl.pallas_call(
        functools.partial(_diffattn_body, tq=tq, tk=tk, seq=seq),
        grid=(batch, B_HEADS, nq),
        in_specs=[
            pl.BlockSpec(memory_space=pltpu.SMEM),
            pl.BlockSpec((tq, HEAD_DIM), lambda b, h, i: (b * nq + i, _BQ_BLK + h)),
            pl.BlockSpec((seq, HEAD_DIM), lambda b, h, i: (b, _BK_BLK + h)),
            pl.BlockSpec((seq, HEAD_DIM), lambda b, h, i: (b, _BV_BLK + h)),
            small((1, B_QK_DIM)), small((1, B_QK_DIM)), small((1, B_QK_DIM)), small((1, B_QK_DIM)),
            small((1, B_V_DIM)),
        ],
        out_specs=pl.BlockSpec((tq, B_V_DIM), lambda b, h, i: (b * nq + i, h)),
        out_shape=jax.ShapeDtypeStruct((t, B_V_W), _BF16),
        scratch_shapes=[
            pltpu.VMEM((3, 2, tq, HEAD_DIM), _BF16),
            pltpu.VMEM((2, seq, HEAD_DIM), _BF16),
            pltpu.VMEM((nk, B_V_DIM + _ONES_ROWS, tk), _BF16),
            pltpu.VMEM((1, 1), _F32),
            pltpu.VMEM((tk, 2 * tq), _F32),
            pltpu.VMEM((tk, 2 * tq), _F32),
            pltpu.VMEM((1, 2 * tq), _F32),
            pltpu.VMEM((B_V_DIM + _ONES_ROWS, 2 * tq), _F32),
        ],
        compiler_params=pltpu.CompilerParams(
            dimension_semantics=("parallel", "parallel", "arbitrary"),
            vmem_limit_bytes=_VMEM_LIMIT_BYTES),
        name="diffattn",
    )(params_b, proj, proj, proj, lq1, lk1, lq2, lk2, subln_g)


def _outproj_body(oa_ref, ob_ref, w_ref, x_ref, o_ref):
    a = jnp.concatenate([oa_ref[...], ob_ref[...]], axis=1)
    o_ref[...] = x_ref[...] + jnp.dot(a, w_ref[...], preferred_element_type=_F32)


def _outproj(oa, ob, w, x2d, *, tm=512):
    t = x2d.shape[0]
    return pl.pallas_call(
        _outproj_body,
        grid=(t // tm,),
        in_specs=[
            pl.BlockSpec((tm, A_Q_W), lambda i: (i, 0)),
            pl.BlockSpec((tm, B_V_W), lambda i: (i, 0)),
            pl.BlockSpec((MIX_WIDTH, D_MODEL), lambda i: (0, 0)),
            pl.BlockSpec((tm, D_MODEL), lambda i: (i, 0)),
        ],
        out_specs=pl.BlockSpec((tm, D_MODEL), lambda i: (i, 0)),
        out_shape=jax.ShapeDtypeStruct((t, D_MODEL), _F32),
        compiler_params=pltpu.CompilerParams(
            dimension_semantics=("parallel",),
            vmem_limit_bytes=_VMEM_LIMIT_BYTES),
        name="outproj",
    )(oa, ob, w, x2d)


def _mlp_body(x_ref, gm_ref, wu_ref, wd_ref, gf_ref, o_ref, h_ref, acc_ref):
    j = pl.program_id(1)

    @pl.when(j == 0)
    def _():
        x = x_ref[...]
        h_ref[...] = (_rms_scale(x) * gm_ref[...]).astype(_BF16)
        acc_ref[...] = x

    u = jnp.maximum(jnp.dot(h_ref[...], wu_ref[...], preferred_element_type=_F32), 0.0)
    acc_ref[...] += jnp.dot((u * u).astype(_BF16), wd_ref[...], preferred_element_type=_F32)

    @pl.when(j == pl.num_programs(1) - 1)
    def _():
        o_ref[...] = _rms_scale(acc_ref[...]) * gf_ref[...]


def _mlp(x1, g_mlp, w_up, w_down, g_final, *, tm=1024, tf=512):
    t = x1.shape[0]
    return pl.pallas_call(
        _mlp_body,
        grid=(t // tm, D_FF // tf),
        in_specs=[
            pl.BlockSpec((tm, D_MODEL), lambda i, j: (i, 0)),
            pl.BlockSpec((1, D_MODEL), lambda i, j: (0, 0)),
            pl.BlockSpec((D_MODEL, tf), lambda i, j: (0, j)),
            pl.BlockSpec((tf, D_MODEL), lambda i, j: (j, 0)),
            pl.BlockSpec((1, D_MODEL), lambda i, j: (0, 0)),
        ],
        out_specs=pl.BlockSpec((tm, D_MODEL), lambda i, j: (i, 0)),
        out_shape=jax.ShapeDtypeStruct((t, D_MODEL), _F32),
        scratch_shapes=[pltpu.VMEM((tm, D_MODEL), _BF16), pltpu.VMEM((tm, D_MODEL), _F32)],
        compiler_params=pltpu.CompilerParams(
            dimension_semantics=("parallel", "arbitrary"),
            vmem_limit_bytes=_VMEM_LIMIT_BYTES),
        name="mlp",
    )(x1, g_mlp, w_up, w_down, g_final)


def _alibi_slopes():
    i = jnp.arange(1, N_HEADS_TOTAL + 1, dtype=_F32)
    s = jnp.exp2(-8.0 / N_HEADS_TOTAL * i)
    return s[0::2], s[1::2]


def _encoder(x, p):
    batch, seq = x.shape[0], x.shape[1]
    x2d = x.reshape(batch * seq, D_MODEL)
    proj = _inproj(x2d, p["g_attn"], p["w_in"], p["colscale"])
    oa = _winattn(proj, p["params_a"], batch=batch, seq=seq)
    ob = _diffattn(proj, p["params_b"], p["lq1"], p["lk1"], p["lq2"], p["lk2"], p["subln_g"],
                   batch=batch, seq=seq)
    x1 = _outproj(oa, ob, p["w_out"], x2d)
    y = _mlp(x1, p["g_mlp"], p["w_up"], p["w_down"], p["g_final"])
    return y.reshape(batch, seq, D_MODEL)


def kernel(x_prompt, x_sample, norm_attn_g, w_in, sink_logits, lambda_q1, lambda_k1, lambda_q2, lambda_k2,
           diff_subln_g, w_out, norm_mlp_g, w_up, w_down, norm_final_g):
    slopes_a, slopes_b = _alibi_slopes()
    col = jnp.arange(IN_WIDTH)
    in_bq = (col >= _BQ_BLK * HEAD_DIM) & (col < _BK_BLK * HEAD_DIM)
    p = {
        "g_attn": norm_attn_g[0].reshape(1, D_MODEL).astype(_F32),
        "w_in": w_in[0].astype(_BF16),
        "colscale": jnp.where(in_bq, 1.0 / math.sqrt(B_QK_DIM), 1.0).astype(_F32).reshape(1, IN_WIDTH),
        "params_a": jnp.concatenate([slopes_a, sink_logits[0].astype(_F32)]),
        "params_b": jnp.concatenate([slopes_b, 1.0 / slopes_b]),
        "lq1": lambda_q1[0].reshape(1, B_QK_DIM).astype(_F32),
        "lk1": lambda_k1[0].reshape(1, B_QK_DIM).astype(_F32),
        "lq2": lambda_q2[0].reshape(1, B_QK_DIM).astype(_F32),
        "lk2": lambda_k2[0].reshape(1, B_QK_DIM).astype(_F32),
        "subln_g": diff_subln_g[0].reshape(1, B_V_DIM).astype(_F32),
        "w_out": w_out[0].astype(_BF16),
        "g_mlp": norm_mlp_g[0].reshape(1, D_MODEL).astype(_F32),
        "w_up": w_up[0].astype(_BF16),
        "w_down": w_down[0].astype(_BF16),
        "g_final": norm_final_g.reshape(1, D_MODEL).astype(_F32),
    }
    return (_encoder(x_prompt, p), _encoder(x_sample, p))
```

```python
import functools
import math

import jax
import jax.numpy as jnp
from jax import lax
from jax.experimental import pallas as pl
from jax.experimental.pallas import tpu as pltpu

D_MODEL = 2048
HEAD_DIM = 128
N_HEADS_TOTAL = D_MODEL // HEAD_DIM
A_HEADS = N_HEADS_TOTAL // 2
A_KV_HEADS = 2
A_GROUP = A_HEADS // A_KV_HEADS
WINDOW = 128
BLOCK = 128
B_HEADS = N_HEADS_TOTAL - A_HEADS
B_QK_DIM = HEAD_DIM // 2
B_V_DIM = HEAD_DIM
MIX_WIDTH = A_HEADS * HEAD_DIM + B_HEADS * B_V_DIM
D_FF = 4 * D_MODEL
EPS = 1e-5
A_Q_W = A_HEADS * HEAD_DIM
A_KV_W = A_KV_HEADS * HEAD_DIM
B_QK_W = B_HEADS * 2 * B_QK_DIM
B_V_W = B_HEADS * B_V_DIM
IN_WIDTH = A_Q_W + 2 * A_KV_W + 2 * B_QK_W + B_V_W
LAMBDA_INIT = 0.8 - 0.6 * math.exp(-0.3 * 0)

_AK_BLK = A_Q_W // HEAD_DIM
_AV_BLK = (A_Q_W + A_KV_W) // HEAD_DIM
_BQ_BLK = (A_Q_W + 2 * A_KV_W) // HEAD_DIM
_BK_BLK = _BQ_BLK + B_QK_W // HEAD_DIM
_BV_BLK = _BK_BLK + B_QK_W // HEAD_DIM

_VMEM_LIMIT_BYTES = 56 * 1024 * 1024

_NT = (((1,), (1,)), ((), ()))
_TN = (((0,), (0,)), ((), ()))

_BF16 = jnp.bfloat16
_F32 = jnp.float32
_LOG2E = math.log2(math.e)


def _rms_scale(x):
    return x * lax.rsqrt(jnp.mean(x * x, axis=-1, keepdims=True) + EPS)


def _inproj_body(x_ref, g_ref, w_ref, cs_ref, o_ref, h_ref):
    @pl.when(pl.program_id(1) == 0)
    def _():
        h_ref[...] = (_rms_scale(x_ref[...]) * g_ref[...]).astype(_BF16)

    acc = jnp.dot(h_ref[...], w_ref[...], preferred_element_type=_F32)
    o_ref[...] = (acc * cs_ref[...]).astype(o_ref.dtype)


def _inproj(x2d, g, w, colscale, *, tm=1024, tn=1536):
    t = x2d.shape[0]
    return pl.pallas_call(
        _inproj_body,
        grid=(t // tm, IN_WIDTH // tn),
        in_specs=[
            pl.BlockSpec((tm, D_MODEL), lambda i, j: (i, 0)),
            pl.BlockSpec((1, D_MODEL), lambda i, j: (0, 0)),
            pl.BlockSpec((D_MODEL, tn), lambda i, j: (0, j)),
            pl.BlockSpec((1, tn), lambda i, j: (0, j)),
        ],
        out_specs=pl.BlockSpec((tm, tn), lambda i, j: (i, j)),
        out_shape=jax.ShapeDtypeStruct((t, IN_WIDTH), _BF16),
        scratch_shapes=[pltpu.VMEM((tm, D_MODEL), _BF16)],
        compiler_params=pltpu.CompilerParams(
            dimension_semantics=("parallel", "arbitrary"),
            vmem_limit_bytes=_VMEM_LIMIT_BYTES),
        name="inproj",
    )(x2d, g, w, colscale)


_WIN_QBLOCKS = 8


def _winattn_body(par_ref, q_ref, kp_ref, kc_ref, kn_ref, vp_ref, vc_ref, vn_ref, o_ref, *, seq):
    i = pl.program_id(1)
    kv = pl.program_id(2)
    kall = jnp.concatenate([kp_ref[...], kc_ref[...], kn_ref[...]], axis=0)
    vall = jnp.concatenate([vp_ref[...], vc_ref[...], vn_ref[...]], axis=0)
    qi = lax.broadcasted_iota(jnp.int32, (BLOCK, 3 * BLOCK), 0)
    kj = lax.broadcasted_iota(jnp.int32, (BLOCK, 3 * BLOCK), 1)
    dist = jnp.abs(kj - BLOCK - qi)
    distf = dist.astype(_F32)
    in_window = dist <= WINDOW
    krow = lax.broadcasted_iota(jnp.int32, (1, 3 * BLOCK), 1)
    bias2, sink2 = [], []
    for g in range(A_GROUP):
        slope = par_ref[kv * A_GROUP + g]
        bias2.append(jnp.where(in_window, (-_LOG2E * slope) * distf, -jnp.inf))
        sink2.append(_LOG2E * par_ref[A_HEADS + kv * A_GROUP + g])
    def scores(blk):
        rows = slice(blk * BLOCK, (blk + 1) * BLOCK)
        q = jnp.concatenate([q_ref[rows, g * HEAD_DIM:(g + 1) * HEAD_DIM] for g in range(A_GROUP)], axis=0)
        return lax.dot_general(q, kall[blk * BLOCK:(blk + 3) * BLOCK], _NT, preferred_element_type=_F32)

    def softmax(blk, s):
        kpos = (i * _WIN_QBLOCKS + blk - 1) * BLOCK + krow
        edge2 = jnp.where((kpos >= 0) & (kpos < seq), 0.0, -jnp.inf)
        ps = []
        for g in range(A_GROUP):
            t = s[g * BLOCK:(g + 1) * BLOCK] * (_LOG2E / math.sqrt(HEAD_DIM)) + bias2[g] + edge2
            m = jnp.maximum(jnp.max(t, axis=-1, keepdims=True), sink2[g])
            p = jnp.exp2(t - m)
            den = jnp.sum(p, axis=-1, keepdims=True) + jnp.exp2(sink2[g] - m)
            ps.append((p * (1.0 / den)).astype(_BF16))
        return jnp.concatenate(ps, axis=0)

    def values(blk, p):
        rows = slice(blk * BLOCK, (blk + 1) * BLOCK)
        o = jnp.dot(p, vall[blk * BLOCK:(blk + 3) * BLOCK], preferred_element_type=_F32)
        for g in range(A_GROUP):
            o_ref[rows, g * HEAD_DIM:(g + 1) * HEAD_DIM] = o[g * BLOCK:(g + 1) * BLOCK].astype(o_ref.dtype)

    s_cur = scores(0)
    p_prev = None
    for blk in range(_WIN_QBLOCKS):
        s_next = scores(blk + 1) if blk + 1 < _WIN_QBLOCKS else None
        p_cur = softmax(blk, s_cur)
        if p_prev is not None:
            values(blk - 1, p_prev)
        s_cur, p_prev = s_next, p_cur
    values(_WIN_QBLOCKS - 1, p_prev)


def _winattn(proj, params_a, *, batch, seq):
    t = batch * seq
    nb = seq // BLOCK
    qb = _WIN_QBLOCKS
    assert nb % qb == 0
    nq = nb // qb
    qw = A_GROUP * HEAD_DIM

    def edge_spec(col0, first):
        def imap(b, i, kv):
            n = jnp.clip(i * qb - 1, 0, nb - 1) if first else jnp.clip((i + 1) * qb, 0, nb - 1)
            return (b * nb + n, col0 + kv)
        return pl.BlockSpec((BLOCK, HEAD_DIM), imap)

    def mid_spec(col0):
        return pl.BlockSpec((qb * BLOCK, HEAD_DIM), lambda b, i, kv: (b * nq + i, col0 + kv))

    return pl.pallas_call(
        functools.partial(_winattn_body, seq=seq),
        grid=(batch, nq, A_KV_HEADS),
        in_specs=[
            pl.BlockSpec(memory_space=pltpu.SMEM),
            pl.BlockSpec((qb * BLOCK, qw), lambda b, i, kv: (b * nq + i, kv)),
            edge_spec(_AK_BLK, True), mid_spec(_AK_BLK), edge_spec(_AK_BLK, False),
            edge_spec(_AV_BLK, True), mid_spec(_AV_BLK), edge_spec(_AV_BLK, False),
        ],
        out_specs=pl.BlockSpec((qb * BLOCK, qw), lambda b, i, kv: (b * nq + i, kv)),
        out_shape=jax.ShapeDtypeStruct((t, A_Q_W), _BF16),
        compiler_params=pltpu.CompilerParams(
            dimension_semantics=("parallel", "parallel", "parallel")),
        name="winattn",
    )(params_a, proj, proj, proj, proj, proj, proj, proj)


_ONES_ROWS = 16
_N_SPLIT = 3
_EXP_ZERO = 110.0
_NORM_SLACK = 2.1


def _split_bf16(t):
    pieces = []
    for _ in range(_N_SPLIT):
        piece = t.astype(_BF16)
        pieces.append(piece)
        t = t - piece.astype(_F32)
    return pieces


def _diffattn_body(par_ref, q_ref, k_ref, v_ref, lq1_ref, lk1_ref, lq2_ref, lk2_ref, g_ref, o_ref,
                   qm_ref, ka_ref, vt_ref, kn_ref, nb_ref, a0_ref, a1_ref, m_ref, acc_ref, *, tq, tk, seq):
    h = pl.program_id(1)
    qi = pl.program_id(2)
    nk = seq // tk
    slope = par_ref[h]
    inv_slope = par_ref[B_HEADS + h]
    lane_k = lax.broadcasted_iota(jnp.int32, (tk, HEAD_DIM), 1)

    def half_sq_norms_max(n2):
        return jnp.max(jnp.max(n2, axis=0, keepdims=True), axis=1, keepdims=True)

    def max_sq_norm(x):
        sq = x.astype(_F32) ** 2
        lane = lax.broadcasted_iota(jnp.int32, sq.shape, 1)
        s0 = jnp.sum(jnp.where(lane < B_QK_DIM, sq, 0.0), axis=1, keepdims=True)
        s1 = jnp.sum(jnp.where(lane >= B_QK_DIM, sq, 0.0), axis=1, keepdims=True)
        return half_sq_norms_max(jnp.maximum(s0, s1))

    def max_sq_norm_mxu(x):
        sq = (x.astype(_F32) ** 2).astype(_BF16)
        row = lax.broadcasted_iota(jnp.int32, (HEAD_DIM, HEAD_DIM), 0)
        colm = lax.broadcasted_iota(jnp.int32, (HEAD_DIM, HEAD_DIM), 1)
        half_sum = jnp.where((row // B_QK_DIM) == colm, 1.0, 0.0).astype(_BF16)
        return half_sq_norms_max(jnp.dot(sq, half_sum, preferred_element_type=_F32))

    @pl.when(qi == 0)
    def _():
        r = lax.broadcasted_iota(jnp.int32, (tk, HEAD_DIM), 0).astype(_F32)
        feats = []
        for mp in range(2):
            base = (1 - mp) * B_QK_DIM
            f = jnp.zeros((tk, HEAD_DIM), _F32)
            for c, piece in enumerate(_split_bf16(slope * r)):
                f = jnp.where(lane_k == base + c, piece.astype(_F32), f)
            feats.append(f)
        ones = jnp.ones((_ONES_ROWS, tk), _BF16)
        kn2 = jnp.zeros((1, 1), _F32)
        for t in range(nk):
            vt_ref[t] = jnp.concatenate([v_ref[t * tk:(t + 1) * tk, :].T, ones], axis=0)
            k = k_ref[t * tk:(t + 1) * tk, :]
            kn2 = jnp.maximum(kn2, max_sq_norm_mxu(k))
            ka_ref[0, t * tk:(t + 1) * tk, :] = jnp.where(lane_k < B_QK_DIM, k.astype(_F32), feats[0]).astype(_BF16)
            ka_ref[1, t * tk:(t + 1) * tk, :] = jnp.where(lane_k >= B_QK_DIM, k.astype(_F32), feats[1]).astype(_BF16)
        kn_ref[...] = kn2
        if tq == tk:
            krow = lax.broadcasted_iota(jnp.int32, (tk, tq), 0)
            qcol = lax.broadcasted_iota(jnp.int32, (tk, tq), 1)
            nb_ref[...] = slope * jnp.abs(qcol - krow).astype(_F32)

    q = q_ref[...]
    lane = lax.broadcasted_iota(jnp.int32, (tq, HEAD_DIM), 1)
    first_half = lane < B_QK_DIM
    for mp in range(2):
        own = first_half if mp == 0 else jnp.logical_not(first_half)
        base = (1 - mp) * B_QK_DIM
        sel = jnp.where((lane >= base) & (lane < base + _N_SPLIT), 1.0, 0.0)
        for kind, cols in enumerate((sel, -sel, jnp.zeros_like(sel))):
            qm_ref[kind, mp] = jnp.where(own, q.astype(_F32), cols).astype(_BF16)
    m_ref[...] = jnp.full_like(m_ref, -jnp.inf)
    acc_ref[...] = jnp.zeros_like(acc_ref)

    col = lax.broadcasted_iota(jnp.int32, (1, 2 * tq), 1)
    qpos = (qi * tq + jnp.where(col < tq, col, col - tq)).astype(_F32)

    def update(a, c, vt):
        m_old = m_ref[...]
        m_new = jnp.maximum(m_old, jnp.max(a, axis=0, keepdims=True) + c)
        alpha = jnp.exp(m_old - m_new)
        p = jnp.exp(a - (m_new - c))
        pv = jnp.dot(vt, p.astype(_BF16), preferred_element_type=_F32)
        acc_ref[...] = alpha * acc_ref[...] + pv
        m_ref[...] = m_new

    q0 = qi * tq
    kd = q0 // tk
    reach = (_EXP_ZERO + _NORM_SLACK * jnp.sqrt(max_sq_norm(q) * kn_ref[...])) * inv_slope
    reach = jnp.where(reach < seq, reach, float(seq)).astype(jnp.int32)[0, 0] + 1
    lo = jnp.clip((q0 - reach) // tk, 0, kd)
    hi = jnp.clip((q0 + tq - 1 + reach) // tk, kd, nk - 1)
    even = (hi - lo) % 2 == 0
    grow_hi = even & (hi < nk - 1)
    hi = hi + grow_hi.astype(jnp.int32)
    lo = lo - (even & jnp.logical_not(grow_hi)).astype(jnp.int32)
    n_far = hi - lo

    def far_index(t):
        ki = lo + t
        ki = ki + (ki >= kd).astype(jnp.int32)
        return ki, (ki > kd).astype(jnp.int32), pl.multiple_of(ki * tk, tk)

    def scores(k0, kind, a_ref):
        for mp in range(2):
            a_ref[:, mp * tq:(mp + 1) * tq] = lax.dot_general(
                ka_ref[mp, pl.ds(k0, tk), :], qm_ref[kind, mp], _NT, preferred_element_type=_F32)

    def far_scores(t, a_ref):
        _, after, k0 = far_index(t)
        scores(k0, after, a_ref)

    def far_fold(t, a_ref):
        ki, after, k0 = far_index(t)
        sign = (1 - 2 * after).astype(_F32)
        c = (sign * slope) * (k0.astype(_F32) - qpos)
        update(a_ref[...], c, vt_ref[ki])

    def near_scores(a_ref):
        scores(pl.multiple_of(kd * tk, tk), 2, a_ref)

    def near_fold(a_ref):
        if tq == tk:
            bias = nb_ref[...]
        else:
            krow = lax.broadcasted_iota(jnp.int32, (tk, tq), 0)
            qcol = lax.broadcasted_iota(jnp.int32, (tk, tq), 1)
            bias = slope * jnp.abs(qcol - krow + (q0 - kd * tk)).astype(_F32)
        update(a_ref[...] - jnp.concatenate([bias, bias], axis=1), jnp.zeros((1, 2 * tq), _F32), vt_ref[kd])

    near_scores(a0_ref)
    far_scores(jnp.int32(0), a1_ref)
    near_fold(a0_ref)

    def far_pair(j, carry):
        t = 2 * j
        far_scores(t + 1, a0_ref)
        far_fold(t, a1_ref)
        far_scores(t + 2, a1_ref)
        far_fold(t + 1, a0_ref)
        return carry

    def far_pairs(count):
        def body(j, carry):
            for u in range(count):
                carry = far_pair(count * j + u, carry)
            return carry
        return body

    n_pairs = (n_far - 1) // 2
    done = jnp.int32(0)
    for count in (4, 2, 1):
        trips = (n_pairs - done) // count
        lax.fori_loop(done // count, done // count + trips, far_pairs(count), 0)
        done = done + trips * count
    far_fold(n_far - 1, a1_ref)

    lam = (jnp.exp(jnp.sum(lq1_ref[...] * lk1_ref[...], axis=-1, keepdims=True))
           - jnp.exp(jnp.sum(lq2_ref[...] * lk2_ref[...], axis=-1, keepdims=True))
           + LAMBDA_INIT)
    acc = acc_ref[...]
    on = acc[0:B_V_DIM] / acc[B_V_DIM:B_V_DIM + 1]
    o = on[:, 0:tq] - lam * on[:, tq:2 * tq]
    y = o * lax.rsqrt(jnp.mean(o * o, axis=0, keepdims=True) + EPS)
    o_ref[...] = ((y.T * g_ref[...]) * (1.0 - LAMBDA_INIT)).astype(o_ref.dtype)


def _diffattn(proj, params_b, lq1, lk1, lq2, lk2, subln_g, *, batch, seq, tq=512, tk=512):
    t = batch * seq
    nq = seq // tq
    nk = seq // tk
    assert tk % tq == 0 and seq % tk == 0 and nk % 2 == 0 and nk >= 2
    small = lambda shape: pl.BlockSpec(shape, lambda b, h, i: (0, 0))
    return pl.pallas_call(
        functools.partial(_diffattn_body, tq=tq, tk=tk, seq=seq),
        grid=(batch, B_HEADS, nq),
        in_specs=[
            pl.BlockSpec(memory_space=pltpu.SMEM),
            pl.BlockSpec((tq, HEAD_DIM), lambda b, h, i: (b * nq + i, _BQ_BLK + h)),
            pl.BlockSpec((seq, HEAD_DIM), lambda b, h, i: (b, _BK_BLK + h)),
            pl.BlockSpec((seq, HEAD_DIM), lambda b, h, i: (b, _BV_BLK + h)),
            small((1, B_QK_DIM)), small((1, B_QK_DIM)), small((1, B_QK_DIM)), small((1, B_QK_DIM)),
            small((1, B_V_DIM)),
        ],
        out_specs=pl.BlockSpec((tq, B_V_DIM), lambda b, h, i: (b * nq + i, h)),
        out_shape=jax.ShapeDtypeStruct((t, B_V_W), _BF16),
        scratch_shapes=[
            pltpu.VMEM((3, 2, tq, HEAD_DIM), _BF16),
            pltpu.VMEM((2, seq, HEAD_DIM), _BF16),
            pltpu.VMEM((nk, B_V_DIM + _ONES_ROWS, tk), _BF16),
            pltpu.VMEM((1, 1), _F32),
            pltpu.VMEM((tk, tq) if tq == tk else (8, 128), _F32),
            pltpu.VMEM((tk, 2 * tq), _F32),
            pltpu.VMEM((tk, 2 * tq), _F32),
            pltpu.VMEM((1, 2 * tq), _F32),
            pltpu.VMEM((B_V_DIM + _ONES_ROWS, 2 * tq), _F32),
        ],
        compiler_params=pltpu.CompilerParams(
            dimension_semantics=("parallel", "parallel", "arbitrary"),
            vmem_limit_bytes=_VMEM_LIMIT_BYTES),
        name="diffattn",
    )(params_b, proj, proj, proj, lq1, lk1, lq2, lk2, subln_g)


def _outproj_body(oa_ref, ob_ref, w_ref, x_ref, o_ref):
    a = jnp.concatenate([oa_ref[...], ob_ref[...]], axis=1)
    o_ref[...] = x_ref[...] + jnp.dot(a, w_ref[...], preferred_element_type=_F32)


def _outproj(oa, ob, w, x2d, *, tm=512):
    t = x2d.shape[0]
    return pl.pallas_call(
        _outproj_body,
        grid=(t // tm,),
        in_specs=[
            pl.BlockSpec((tm, A_Q_W), lambda i: (i, 0)),
            pl.BlockSpec((tm, B_V_W), lambda i: (i, 0)),
            pl.BlockSpec((MIX_WIDTH, D_MODEL), lambda i: (0, 0)),
            pl.BlockSpec((tm, D_MODEL), lambda i: (i, 0)),
        ],
        out_specs=pl.BlockSpec((tm, D_MODEL), lambda i: (i, 0)),
        out_shape=jax.ShapeDtypeStruct((t, D_MODEL), _F32),
        compiler_params=pltpu.CompilerParams(
            dimension_semantics=("parallel",),
            vmem_limit_bytes=_VMEM_LIMIT_BYTES),
        name="outproj",
    )(oa, ob, w, x2d)


def _mlp_body(x_ref, gm_ref, wu_ref, wd_ref, gf_ref, o_ref, h_ref, acc_ref):
    j = pl.program_id(1)

    @pl.when(j == 0)
    def _():
        x = x_ref[...]
        h_ref[...] = (_rms_scale(x) * gm_ref[...]).astype(_BF16)
        acc_ref[...] = x

    u = jnp.maximum(jnp.dot(h_ref[...], wu_ref[...], preferred_element_type=_F32), 0.0)
    acc_ref[...] += jnp.dot((u * u).astype(_BF16), wd_ref[...], preferred_element_type=_F32)

    @pl.when(j == pl.num_programs(1) - 1)
    def _():
        o_ref[...] = _rms_scale(acc_ref[...]) * gf_ref[...]


def _mlp(x1, g_mlp, w_up, w_down, g_final, *, tm=512, tf=1024):
    t = x1.shape[0]
    return pl.pallas_call(
        _mlp_body,
        grid=(t // tm, D_FF // tf),
        in_specs=[
            pl.BlockSpec((tm, D_MODEL), lambda i, j: (i, 0)),
            pl.BlockSpec((1, D_MODEL), lambda i, j: (0, 0)),
            pl.BlockSpec((D_MODEL, tf), lambda i, j: (0, j)),
            pl.BlockSpec((tf, D_MODEL), lambda i, j: (j, 0)),
            pl.BlockSpec((1, D_MODEL), lambda i, j: (0, 0)),
        ],
        out_specs=pl.BlockSpec((tm, D_MODEL), lambda i, j: (i, 0)),
        out_shape=jax.ShapeDtypeStruct((t, D_MODEL), _F32),
        scratch_shapes=[pltpu.VMEM((tm, D_MODEL), _BF16), pltpu.VMEM((tm, D_MODEL), _F32)],
        compiler_params=pltpu.CompilerParams(
            dimension_semantics=("parallel", "arbitrary"),
            vmem_limit_bytes=_VMEM_LIMIT_BYTES),
        name="mlp",
    )(x1, g_mlp, w_up, w_down, g_final)


def _alibi_slopes():
    i = jnp.arange(1, N_HEADS_TOTAL + 1, dtype=_F32)
    s = jnp.exp2(-8.0 / N_HEADS_TOTAL * i)
    return s[0::2], s[1::2]


def _encoder(x, p):
    batch, seq = x.shape[0], x.shape[1]
    x2d = x.reshape(batch * seq, D_MODEL)
    proj = _inproj(x2d, p["g_attn"], p["w_in"], p["colscale"])
    oa = _winattn(proj, p["params_a"], batch=batch, seq=seq)
    ob = _diffattn(proj, p["params_b"], p["lq1"], p["lk1"], p["lq2"], p["lk2"], p["subln_g"],
                   batch=batch, seq=seq)
    x1 = _outproj(oa, ob, p["w_out"], x2d)
    y = _mlp(x1, p["g_mlp"], p["w_up"], p["w_down"], p["g_final"])
    return y.reshape(batch, seq, D_MODEL)


def kernel(x_prompt, x_sample, norm_attn_g, w_in, sink_logits, lambda_q1, lambda_k1, lambda_q2, lambda_k2,
           diff_subln_g, w_out, norm_mlp_g, w_up, w_down, norm_final_g):
    slopes_a, slopes_b = _alibi_slopes()
    col = jnp.arange(IN_WIDTH)
    in_bq = (col >= _BQ_BLK * HEAD_DIM) & (col < _BK_BLK * HEAD_DIM)
    p = {
        "g_attn": norm_attn_g[0].reshape(1, D_MODEL).astype(_F32),
        "w_in": w_in[0].astype(_BF16),
        "colscale": jnp.where(in_bq, 1.0 / math.sqrt(B_QK_DIM), 1.0).astype(_F32).reshape(1, IN_WIDTH),
        "params_a": jnp.concatenate([slopes_a, sink_logits[0].astype(_F32)]),
        "params_b": jnp.concatenate([slopes_b, 1.0 / slopes_b]),
        "lq1": lambda_q1[0].reshape(1, B_QK_DIM).astype(_F32),
        "lk1": lambda_k1[0].reshape(1, B_QK_DIM).astype(_F32),
        "lq2": lambda_q2[0].reshape(1, B_QK_DIM).astype(_F32),
        "lk2": lambda_k2[0].reshape(1, B_QK_DIM).astype(_F32),
        "subln_g": diff_subln_g[0].reshape(1, B_V_DIM).astype(_F32),
        "w_out": w_out[0].astype(_BF16),
        "g_mlp": norm_mlp_g[0].reshape(1, D_MODEL).astype(_F32),
        "w_up": w_up[0].astype(_BF16),
        "w_down": w_down[0].astype(_BF16),
        "g_final": norm_final_g.reshape(1, D_MODEL).astype(_F32),
    }
    return (_encoder(x_prompt, p), _encoder(x_sample, p))
```

```python
import functools
import math

import jax
import jax.numpy as jnp
from jax import lax
from jax.experimental import pallas as pl
from jax.experimental.pallas import tpu as pltpu

D_MODEL = 2048
HEAD_DIM = 128
N_HEADS_TOTAL = D_MODEL // HEAD_DIM
A_HEADS = N_HEADS_TOTAL // 2
A_KV_HEADS = 2
A_GROUP = A_HEADS // A_KV_HEADS
WINDOW = 128
BLOCK = 128
B_HEADS = N_HEADS_TOTAL - A_HEADS
B_QK_DIM = HEAD_DIM // 2
B_V_DIM = HEAD_DIM
MIX_WIDTH = A_HEADS * HEAD_DIM + B_HEADS * B_V_DIM
D_FF = 4 * D_MODEL
EPS = 1e-5
A_Q_W = A_HEADS * HEAD_DIM
A_KV_W = A_KV_HEADS * HEAD_DIM
B_QK_W = B_HEADS * 2 * B_QK_DIM
B_V_W = B_HEADS * B_V_DIM
IN_WIDTH = A_Q_W + 2 * A_KV_W + 2 * B_QK_W + B_V_W
LAMBDA_INIT = 0.8 - 0.6 * math.exp(-0.3 * 0)

_AK_BLK = A_Q_W // HEAD_DIM
_AV_BLK = (A_Q_W + A_KV_W) // HEAD_DIM
_BQ_BLK = (A_Q_W + 2 * A_KV_W) // HEAD_DIM
_BK_BLK = _BQ_BLK + B_QK_W // HEAD_DIM
_BV_BLK = _BK_BLK + B_QK_W // HEAD_DIM

_VMEM_LIMIT_BYTES = 56 * 1024 * 1024

_NT = (((1,), (1,)), ((), ()))

_BF16 = jnp.bfloat16
_F32 = jnp.float32
_LOG2E = math.log2(math.e)


def _rms_scale(x):
    return x * lax.rsqrt(jnp.mean(x * x, axis=-1, keepdims=True) + EPS)


def _inproj_body(x_ref, g_ref, w_ref, cs_ref, o_ref, h_ref):
    @pl.when(pl.program_id(1) == 0)
    def _():
        h_ref[...] = (_rms_scale(x_ref[...]) * g_ref[...]).astype(_BF16)

    acc = jnp.dot(h_ref[...], w_ref[...], preferred_element_type=_F32)
    o_ref[...] = (acc * cs_ref[...]).astype(o_ref.dtype)


def _inproj(x2d, g, w, colscale, *, tm=1024, tn=1536):
    t = x2d.shape[0]
    return pl.pallas_call(
        _inproj_body,
        grid=(t // tm, IN_WIDTH // tn),
        in_specs=[
            pl.BlockSpec((tm, D_MODEL), lambda i, j: (i, 0)),
            pl.BlockSpec((1, D_MODEL), lambda i, j: (0, 0)),
            pl.BlockSpec((D_MODEL, tn), lambda i, j: (0, j)),
            pl.BlockSpec((1, tn), lambda i, j: (0, j)),
        ],
        out_specs=pl.BlockSpec((tm, tn), lambda i, j: (i, j)),
        out_shape=jax.ShapeDtypeStruct((t, IN_WIDTH), _BF16),
        scratch_shapes=[pltpu.VMEM((tm, D_MODEL), _BF16)],
        compiler_params=pltpu.CompilerParams(
            dimension_semantics=("parallel", "arbitrary"),
            vmem_limit_bytes=_VMEM_LIMIT_BYTES),
        name="inproj",
    )(x2d, g, w, colscale)


_WIN_QBLOCKS = 8


def _winattn_body(par_ref, q_ref, kp_ref, kc_ref, kn_ref, vp_ref, vc_ref, vn_ref, o_ref, *, seq):
    i = pl.program_id(1)
    kv = pl.program_id(2)
    kall = jnp.concatenate([kp_ref[...], kc_ref[...], kn_ref[...]], axis=0)
    vall = jnp.concatenate([vp_ref[...], vc_ref[...], vn_ref[...]], axis=0)
    qi = lax.broadcasted_iota(jnp.int32, (BLOCK, 3 * BLOCK), 0)
    kj = lax.broadcasted_iota(jnp.int32, (BLOCK, 3 * BLOCK), 1)
    dist = jnp.abs(kj - BLOCK - qi)
    distf = dist.astype(_F32)
    in_window = dist <= WINDOW
    krow = lax.broadcasted_iota(jnp.int32, (1, 3 * BLOCK), 1)
    bias2, sink2 = [], []
    for g in range(A_GROUP):
        slope = par_ref[kv * A_GROUP + g]
        bias2.append(jnp.where(in_window, (-_LOG2E * slope) * distf, -jnp.inf))
        sink2.append(_LOG2E * par_ref[A_HEADS + kv * A_GROUP + g])

    def scores(blk):
        rows = slice(blk * BLOCK, (blk + 1) * BLOCK)
        q = jnp.concatenate([q_ref[rows, g * HEAD_DIM:(g + 1) * HEAD_DIM] for g in range(A_GROUP)], axis=0)
        return lax.dot_general(q, kall[blk * BLOCK:(blk + 3) * BLOCK], _NT, preferred_element_type=_F32)

    def softmax(blk, s):
        kpos = (i * _WIN_QBLOCKS + blk - 1) * BLOCK + krow
        edge2 = jnp.where((kpos >= 0) & (kpos < seq), 0.0, -jnp.inf)
        ps = []
        for g in range(A_GROUP):
            t = s[g * BLOCK:(g + 1) * BLOCK] * (_LOG2E / math.sqrt(HEAD_DIM)) + bias2[g] + edge2
            m = jnp.maximum(jnp.max(t, axis=-1, keepdims=True), sink2[g])
            p = jnp.exp2(t - m)
            den = jnp.sum(p, axis=-1, keepdims=True) + jnp.exp2(sink2[g] - m)
            ps.append((p * (1.0 / den)).astype(_BF16))
        return jnp.concatenate(ps, axis=0)

    def values(blk, p):
        rows = slice(blk * BLOCK, (blk + 1) * BLOCK)
        o = jnp.dot(p, vall[blk * BLOCK:(blk + 3) * BLOCK], preferred_element_type=_F32)
        for g in range(A_GROUP):
            o_ref[rows, g * HEAD_DIM:(g + 1) * HEAD_DIM] = o[g * BLOCK:(g + 1) * BLOCK].astype(o_ref.dtype)

    s_cur = scores(0)
    p_prev = None
    for blk in range(_WIN_QBLOCKS):
        s_next = scores(blk + 1) if blk + 1 < _WIN_QBLOCKS else None
        p_cur = softmax(blk, s_cur)
        if p_prev is not None:
            values(blk - 1, p_prev)
        s_cur, p_prev = s_next, p_cur
    values(_WIN_QBLOCKS - 1, p_prev)


def _winattn(proj, params_a, *, batch, seq):
    t = batch * seq
    nb = seq // BLOCK
    qb = _WIN_QBLOCKS
    assert nb % qb == 0
    nq = nb // qb
    qw = A_GROUP * HEAD_DIM

    def edge_spec(col0, first):
        def imap(b, i, kv):
            n = jnp.clip(i * qb - 1, 0, nb - 1) if first else jnp.clip((i + 1) * qb, 0, nb - 1)
            return (b * nb + n, col0 + kv)
        return pl.BlockSpec((BLOCK, HEAD_DIM), imap)

    def mid_spec(col0):
        return pl.BlockSpec((qb * BLOCK, HEAD_DIM), lambda b, i, kv: (b * nq + i, col0 + kv))

    return pl.pallas_call(
        functools.partial(_winattn_body, seq=seq),
        grid=(batch, nq, A_KV_HEADS),
        in_specs=[
            pl.BlockSpec(memory_space=pltpu.SMEM),
            pl.BlockSpec((qb * BLOCK, qw), lambda b, i, kv: (b * nq + i, kv)),
            edge_spec(_AK_BLK, True), mid_spec(_AK_BLK), edge_spec(_AK_BLK, False),
            edge_spec(_AV_BLK, True), mid_spec(_AV_BLK), edge_spec(_AV_BLK, False),
        ],
        out_specs=pl.BlockSpec((qb * BLOCK, qw), lambda b, i, kv: (b * nq + i, kv)),
        out_shape=jax.ShapeDtypeStruct((t, A_Q_W), _BF16),
        compiler_params=pltpu.CompilerParams(
            dimension_semantics=("parallel", "parallel", "parallel")),
        name="winattn",
    )(params_a, proj, proj, proj, proj, proj, proj, proj)


_ONES_ROWS = 16
_N_SPLIT = 3
_EXP_ZERO = 110.0
_NORM_SLACK = 2.1
_Q_SUBTILES = 4


def _split_bf16(t):
    pieces = []
    for _ in range(_N_SPLIT):
        piece = t.astype(_BF16)
        pieces.append(piece)
        t = t - piece.astype(_F32)
    return pieces


def _diffattn_body(par_ref, q_ref, k_ref, v_ref, lq1_ref, lk1_ref, lq2_ref, lk2_ref, g_ref, o_ref,
                   qm_ref, ka_ref, vt_ref, kn_ref, nb_ref, a0_ref, a1_ref, m_ref, acc_ref, *, tq, tk, seq):
    h = pl.program_id(1)
    qi = pl.program_id(2)
    nk = seq // tk
    slope = par_ref[h]
    inv_slope = par_ref[B_HEADS + h]
    lane_k = lax.broadcasted_iota(jnp.int32, (tk, HEAD_DIM), 1)

    def half_sq_norms_max(n2):
        return jnp.max(jnp.max(n2, axis=0, keepdims=True), axis=1, keepdims=True)

    def max_sq_norm(x):
        sq = x.astype(_F32) ** 2
        lane = lax.broadcasted_iota(jnp.int32, sq.shape, 1)
        s0 = jnp.sum(jnp.where(lane < B_QK_DIM, sq, 0.0), axis=1, keepdims=True)
        s1 = jnp.sum(jnp.where(lane >= B_QK_DIM, sq, 0.0), axis=1, keepdims=True)
        return half_sq_norms_max(jnp.maximum(s0, s1))

    def max_sq_norm_mxu(x):
        sq = (x.astype(_F32) ** 2).astype(_BF16)
        row = lax.broadcasted_iota(jnp.int32, (HEAD_DIM, HEAD_DIM), 0)
        colm = lax.broadcasted_iota(jnp.int32, (HEAD_DIM, HEAD_DIM), 1)
        half_sum = jnp.where((row // B_QK_DIM) == colm, 1.0, 0.0).astype(_BF16)
        return half_sq_norms_max(jnp.dot(sq, half_sum, preferred_element_type=_F32))

    @pl.when(qi == 0)
    def _():
        r = lax.broadcasted_iota(jnp.int32, (tk, HEAD_DIM), 0).astype(_F32)
        feats = []
        for mp in range(2):
            base = (1 - mp) * B_QK_DIM
            f = jnp.zeros((tk, HEAD_DIM), _F32)
            for c, piece in enumerate(_split_bf16(slope * r)):
                f = jnp.where(lane_k == base + c, piece.astype(_F32), f)
            feats.append(f)
        ones = jnp.ones((_ONES_ROWS, tk), _BF16)
        kn2 = jnp.zeros((1, 1), _F32)
        for t in range(nk):
            vt_ref[t] = jnp.concatenate([v_ref[t * tk:(t + 1) * tk, :].T, ones], axis=0)
            k = k_ref[t * tk:(t + 1) * tk, :]
            kn2 = jnp.maximum(kn2, max_sq_norm_mxu(k))
            ka_ref[0, t * tk:(t + 1) * tk, :] = jnp.where(lane_k < B_QK_DIM, k.astype(_F32), feats[0]).astype(_BF16)
            ka_ref[1, t * tk:(t + 1) * tk, :] = jnp.where(lane_k >= B_QK_DIM, k.astype(_F32), feats[1]).astype(_BF16)
        kn_ref[...] = kn2
        if tq == tk:
            krow = lax.broadcasted_iota(jnp.int32, (tk, tq), 0)
            qcol = lax.broadcasted_iota(jnp.int32, (tk, tq), 1)
            nb_ref[...] = slope * jnp.abs(qcol - krow).astype(_F32)

    def query_tile(sub):
        q = q_ref[sub * tq:(sub + 1) * tq, :]
        lane = lax.broadcasted_iota(jnp.int32, (tq, HEAD_DIM), 1)
        first_half = lane < B_QK_DIM
        for mp in range(2):
            own = first_half if mp == 0 else jnp.logical_not(first_half)
            base = (1 - mp) * B_QK_DIM
            sel = jnp.where((lane >= base) & (lane < base + _N_SPLIT), 1.0, 0.0)
            for kind, cols in enumerate((sel, -sel, jnp.zeros_like(sel))):
                qm_ref[kind, mp] = jnp.where(own, q.astype(_F32), cols).astype(_BF16)
        m_ref[...] = jnp.full_like(m_ref, -jnp.inf)
        acc_ref[...] = jnp.zeros_like(acc_ref)

        col = lax.broadcasted_iota(jnp.int32, (1, 2 * tq), 1)
        q0 = (qi * _Q_SUBTILES + sub) * tq
        qpos = (q0 + jnp.where(col < tq, col, col - tq)).astype(_F32)

        def update(a, c, vt):
            m_old = m_ref[...]
            m_new = jnp.maximum(m_old, jnp.max(a, axis=0, keepdims=True) + c)
            alpha = jnp.exp(m_old - m_new)
            p = jnp.exp(a - (m_new - c))
            pv = jnp.dot(vt, p.astype(_BF16), preferred_element_type=_F32)
            acc_ref[...] = alpha * acc_ref[...] + pv
            m_ref[...] = m_new

        kd = q0 // tk
        reach = (_EXP_ZERO + _NORM_SLACK * jnp.sqrt(max_sq_norm(q) * kn_ref[...])) * inv_slope
        reach = jnp.where(reach < seq, reach, float(seq)).astype(jnp.int32)[0, 0] + 1
        lo = jnp.clip((q0 - reach) // tk, 0, kd)
        hi = jnp.clip((q0 + tq - 1 + reach) // tk, kd, nk - 1)
        even = (hi - lo) % 2 == 0
        grow_hi = even & (hi < nk - 1)
        hi = hi + grow_hi.astype(jnp.int32)
        lo = lo - (even & jnp.logical_not(grow_hi)).astype(jnp.int32)
        n_far = hi - lo

        def far_index(t):
            ki = lo + t
            ki = ki + (ki >= kd).astype(jnp.int32)
            return ki, (ki > kd).astype(jnp.int32), pl.multiple_of(ki * tk, tk)

        def scores(k0, kind, a_ref):
            for mp in range(2):
                a_ref[:, mp * tq:(mp + 1) * tq] = lax.dot_general(
                    ka_ref[mp, pl.ds(k0, tk), :], qm_ref[kind, mp], _NT, preferred_element_type=_F32)

        def far_scores(t, a_ref):
            _, after, k0 = far_index(t)
            scores(k0, after, a_ref)

        def far_fold(t, a_ref):
            ki, after, k0 = far_index(t)
            sign = (1 - 2 * after).astype(_F32)
            c = (sign * slope) * (k0.astype(_F32) - qpos)
            update(a_ref[...], c, vt_ref[ki])

        def near_scores(a_ref):
            scores(pl.multiple_of(kd * tk, tk), 2, a_ref)

        def near_fold(a_ref):
            if tq == tk:
                bias = nb_ref[...]
            else:
                krow = lax.broadcasted_iota(jnp.int32, (tk, tq), 0)
                qcol = lax.broadcasted_iota(jnp.int32, (tk, tq), 1)
                bias = slope * jnp.abs(qcol - krow + (q0 - kd * tk)).astype(_F32)
            update(a_ref[...] - jnp.concatenate([bias, bias], axis=1), jnp.zeros((1, 2 * tq), _F32), vt_ref[kd])

        near_scores(a0_ref)
        far_scores(jnp.int32(0), a1_ref)
        near_fold(a0_ref)

        def far_pair(j, carry):
            t = 2 * j
            far_scores(t + 1, a0_ref)
            far_fold(t, a1_ref)
            far_scores(t + 2, a1_ref)
            far_fold(t + 1, a0_ref)
            return carry

        def far_pairs(count):
            def body(j, carry):
                for u in range(count):
                    carry = far_pair(count * j + u, carry)
                return carry
            return body

        n_pairs = (n_far - 1) // 2
        done = jnp.int32(0)
        for count in (4, 2, 1):
            trips = (n_pairs - done) // count
            lax.fori_loop(done // count, done // count + trips, far_pairs(count), 0)
            done = done + trips * count
        far_fold(n_far - 1, a1_ref)

        lam = (jnp.exp(jnp.sum(lq1_ref[...] * lk1_ref[...], axis=-1, keepdims=True))
               - jnp.exp(jnp.sum(lq2_ref[...] * lk2_ref[...], axis=-1, keepdims=True))
               + LAMBDA_INIT)
        acc = acc_ref[...]
        on = acc[0:B_V_DIM] / acc[B_V_DIM:B_V_DIM + 1]
        o = on[:, 0:tq] - lam * on[:, tq:2 * tq]
        y = o * lax.rsqrt(jnp.mean(o * o, axis=0, keepdims=True) + EPS)
        o_ref[sub * tq:(sub + 1) * tq, :] = ((y.T * g_ref[...]) * (1.0 - LAMBDA_INIT)).astype(o_ref.dtype)

    for sub in range(_Q_SUBTILES):
        query_tile(sub)


def _diffattn(proj, params_b, lq1, lk1, lq2, lk2, subln_g, *, batch, seq, tq=512, tk=512):
    t = batch * seq
    tqs = tq * _Q_SUBTILES
    nq = seq // tqs
    nk = seq // tk
    assert tk % tq == 0 and seq % tk == 0 and nk % 2 == 0 and nk >= 2 and seq % tqs == 0
    small = lambda shape: pl.BlockSpec(shape, lambda b, h, i: (0, 0))
    return pl.pallas_call(
        functools.partial(_diffattn_body, tq=tq, tk=tk, seq=seq),
        grid=(batch, B_HEADS, nq),
        in_specs=[
            pl.BlockSpec(memory_space=pltpu.SMEM),
            pl.BlockSpec((tqs, HEAD_DIM), lambda b, h, i: (b * nq + i, _BQ_BLK + h)),
            pl.BlockSpec((seq, HEAD_DIM), lambda b, h, i: (b, _BK_BLK + h)),
            pl.BlockSpec((seq, HEAD_DIM), lambda b, h, i: (b, _BV_BLK + h)),
            small((1, B_QK_DIM)), small((1, B_QK_DIM)), small((1, B_QK_DIM)), small((1, B_QK_DIM)),
            small((1, B_V_DIM)),
        ],
        out_specs=pl.BlockSpec((tqs, B_V_DIM), lambda b, h, i: (b * nq + i, h)),
        out_shape=jax.ShapeDtypeStruct((t, B_V_W), _BF16),
        scratch_shapes=[
            pltpu.VMEM((3, 2, tq, HEAD_DIM), _BF16),
            pltpu.VMEM((2, seq, HEAD_DIM), _BF16),
            pltpu.VMEM((nk, B_V_DIM + _ONES_ROWS, tk), _BF16),
            pltpu.VMEM((1, 1), _F32),
            pltpu.VMEM((tk, tq) if tq == tk else (8, 128), _F32),
            pltpu.VMEM((tk, 2 * tq), _F32),
            pltpu.VMEM((tk, 2 * tq), _F32),
            pltpu.VMEM((1, 2 * tq), _F32),
            pltpu.VMEM((B_V_DIM + _ONES_ROWS, 2 * tq), _F32),
        ],
        compiler_params=pltpu.CompilerParams(
            dimension_semantics=("parallel", "parallel", "arbitrary"),
            vmem_limit_bytes=_VMEM_LIMIT_BYTES),
        name="diffattn",
    )(params_b, proj, proj, proj, lq1, lk1, lq2, lk2, subln_g)


def _outproj_body(oa_ref, ob_ref, w_ref, x_ref, o_ref):
    a = jnp.concatenate([oa_ref[...], ob_ref[...]], axis=1)
    o_ref[...] = x_ref[...] + jnp.dot(a, w_ref[...], preferred_element_type=_F32)


def _outproj(oa, ob, w, x2d, *, tm=512):
    t = x2d.shape[0]
    return pl.pallas_call(
        _outproj_body,
        grid=(t // tm,),
        in_specs=[
            pl.BlockSpec((tm, A_Q_W), lambda i: (i, 0)),
            pl.BlockSpec((tm, B_V_W), lambda i: (i, 0)),
            pl.BlockSpec((MIX_WIDTH, D_MODEL), lambda i: (0, 0)),
            pl.BlockSpec((tm, D_MODEL), lambda i: (i, 0)),
        ],
        out_specs=pl.BlockSpec((tm, D_MODEL), lambda i: (i, 0)),
        out_shape=jax.ShapeDtypeStruct((t, D_MODEL), _F32),
        compiler_params=pltpu.CompilerParams(
            dimension_semantics=("parallel",),
            vmem_limit_bytes=_VMEM_LIMIT_BYTES),
        name="outproj",
    )(oa, ob, w, x2d)


def _mlp_body(x_ref, gm_ref, wu_ref, wd_ref, gf_ref, o_ref, h_ref, acc_ref):
    j = pl.program_id(1)

    @pl.when(j == 0)
    def _():
        x = x_ref[...]
        h_ref[...] = (_rms_scale(x) * gm_ref[...]).astype(_BF16)
        acc_ref[...] = x

    u = jnp.maximum(jnp.dot(h_ref[...], wu_ref[...], preferred_element_type=_F32), 0.0)
    acc_ref[...] += jnp.dot((u * u).astype(_BF16), wd_ref[...], preferred_element_type=_F32)

    @pl.when(j == pl.num_programs(1) - 1)
    def _():
        o_ref[...] = _rms_scale(acc_ref[...]) * gf_ref[...]


def _mlp(x1, g_mlp, w_up, w_down, g_final, *, tm=512, tf=1024):
    t = x1.shape[0]
    return pl.pallas_call(
        _mlp_body,
        grid=(t // tm, D_FF // tf),
        in_specs=[
            pl.BlockSpec((tm, D_MODEL), lambda i, j: (i, 0)),
            pl.BlockSpec((1, D_MODEL), lambda i, j: (0, 0)),
            pl.BlockSpec((D_MODEL, tf), lambda i, j: (0, j)),
            pl.BlockSpec((tf, D_MODEL), lambda i, j: (j, 0)),
            pl.BlockSpec((1, D_MODEL), lambda i, j: (0, 0)),
        ],
        out_specs=pl.BlockSpec((tm, D_MODEL), lambda i, j: (i, 0)),
        out_shape=jax.ShapeDtypeStruct((t, D_MODEL), _F32),
        scratch_shapes=[pltpu.VMEM((tm, D_MODEL), _BF16), pltpu.VMEM((tm, D_MODEL), _F32)],
        compiler_params=pltpu.CompilerParams(
            dimension_semantics=("parallel", "arbitrary"),
            vmem_limit_bytes=_VMEM_LIMIT_BYTES),
        name="mlp",
    )(x1, g_mlp, w_up, w_down, g_final)


def _alibi_slopes():
    i = jnp.arange(1, N_HEADS_TOTAL + 1, dtype=_F32)
    s = jnp.exp2(-8.0 / N_HEADS_TOTAL * i)
    return s[0::2], s[1::2]


def _encoder(x, p):
    batch, seq = x.shape[0], x.shape[1]
    x2d = x.reshape(batch * seq, D_MODEL)
    proj = _inproj(x2d, p["g_attn"], p["w_in"], p["colscale"])
    oa = _winattn(proj, p["params_a"], batch=batch, seq=seq)
    ob = _diffattn(proj, p["params_b"], p["lq1"], p["lk1"], p["lq2"], p["lk2"], p["subln_g"],
                   batch=batch, seq=seq)
    x1 = _outproj(oa, ob, p["w_out"], x2d)
    y = _mlp(x1, p["g_mlp"], p["w_up"], p["w_down"], p["g_final"])
    return y.reshape(batch, seq, D_MODEL)


def kernel(x_prompt, x_sample, norm_attn_g, w_in, sink_logits, lambda_q1, lambda_k1, lambda_q2, lambda_k2,
           diff_subln_g, w_out, norm_mlp_g, w_up, w_down, norm_final_g):
    slopes_a, slopes_b = _alibi_slopes()
    col = jnp.arange(IN_WIDTH)
    in_bq = (col >= _BQ_BLK * HEAD_DIM) & (col < _BK_BLK * HEAD_DIM)
    p = {
        "g_attn": norm_attn_g[0].reshape(1, D_MODEL).astype(_F32),
        "w_in": w_in[0].astype(_BF16),
        "colscale": jnp.where(in_bq, 1.0 / math.sqrt(B_QK_DIM), 1.0).astype(_F32).reshape(1, IN_WIDTH),
        "params_a": jnp.concatenate([slopes_a, sink_logits[0].astype(_F32)]),
        "params_b": jnp.concatenate([slopes_b, 1.0 / slopes_b]),
        "lq1": lambda_q1[0].reshape(1, B_QK_DIM).astype(_F32),
        "lk1": lambda_k1[0].reshape(1, B_QK_DIM).astype(_F32),
        "lq2": lambda_q2[0].reshape(1, B_QK_DIM).astype(_F32),
        "lk2": lambda_k2[0].reshape(1, B_QK_DIM).astype(_F32),
        "subln_g": diff_subln_g[0].reshape(1, B_V_DIM).astype(_F32),
        "w_out": w_out[0].astype(_BF16),
        "g_mlp": norm_mlp_g[0].reshape(1, D_MODEL).astype(_F32),
        "w_up": w_up[0].astype(_BF16),
        "w_down": w_down[0].astype(_BF16),
        "g_final": norm_final_g.reshape(1, D_MODEL).astype(_F32),
    }
    return (_encoder(x_prompt, p), _encoder(x_sample, p))
```

```python
import functools
import math

import jax
import jax.numpy as jnp
from jax import lax
from jax.experimental import pallas as pl
from jax.experimental.pallas import tpu as pltpu

D_MODEL = 2048
HEAD_DIM = 128
N_HEADS_TOTAL = D_MODEL // HEAD_DIM
A_HEADS = N_HEADS_TOTAL // 2
A_KV_HEADS = 2
A_GROUP = A_HEADS // A_KV_HEADS
WINDOW = 128
BLOCK = 128
B_HEADS = N_HEADS_TOTAL - A_HEADS
B_QK_DIM = HEAD_DIM // 2
B_V_DIM = HEAD_DIM
MIX_WIDTH = A_HEADS * HEAD_DIM + B_HEADS * B_V_DIM
D_FF = 4 * D_MODEL
EPS = 1e-5
A_Q_W = A_HEADS * HEAD_DIM
A_KV_W = A_KV_HEADS * HEAD_DIM
B_QK_W = B_HEADS * 2 * B_QK_DIM
B_V_W = B_HEADS * B_V_DIM
IN_WIDTH = A_Q_W + 2 * A_KV_W + 2 * B_QK_W + B_V_W
LAMBDA_INIT = 0.8 - 0.6 * math.exp(-0.3 * 0)

_AK_BLK = A_Q_W // HEAD_DIM
_AV_BLK = (A_Q_W + A_KV_W) // HEAD_DIM
_BQ_BLK = (A_Q_W + 2 * A_KV_W) // HEAD_DIM
_BK_BLK = _BQ_BLK + B_QK_W // HEAD_DIM
_BV_BLK = _BK_BLK + B_QK_W // HEAD_DIM

_VMEM_LIMIT_BYTES = 56 * 1024 * 1024

_NT = (((1,), (1,)), ((), ()))

_BF16 = jnp.bfloat16
_F32 = jnp.float32
_LOG2E = math.log2(math.e)


def _rms_scale(x):
    return x * lax.rsqrt(jnp.mean(x * x, axis=-1, keepdims=True) + EPS)


def _inproj_body(x_ref, g_ref, w_ref, cs_ref, o_ref, h_ref):
    @pl.when(pl.program_id(1) == 0)
    def _():
        h_ref[...] = (_rms_scale(x_ref[...]) * g_ref[...]).astype(_BF16)

    acc = jnp.dot(h_ref[...], w_ref[...], preferred_element_type=_F32)
    o_ref[...] = (acc * cs_ref[...]).astype(o_ref.dtype)


def _inproj(x2d, g, w, colscale, *, tm=1024, tn=1536):
    t = x2d.shape[0]
    return pl.pallas_call(
        _inproj_body,
        grid=(t // tm, IN_WIDTH // tn),
        in_specs=[
            pl.BlockSpec((tm, D_MODEL), lambda i, j: (i, 0)),
            pl.BlockSpec((1, D_MODEL), lambda i, j: (0, 0)),
            pl.BlockSpec((D_MODEL, tn), lambda i, j: (0, j)),
            pl.BlockSpec((1, tn), lambda i, j: (0, j)),
        ],
        out_specs=pl.BlockSpec((tm, tn), lambda i, j: (i, j)),
        out_shape=jax.ShapeDtypeStruct((t, IN_WIDTH), _BF16),
        scratch_shapes=[pltpu.VMEM((tm, D_MODEL), _BF16)],
        compiler_params=pltpu.CompilerParams(
            dimension_semantics=("parallel", "arbitrary"),
            vmem_limit_bytes=_VMEM_LIMIT_BYTES),
        name="inproj",
    )(x2d, g, w, colscale)


_WIN_QBLOCKS = 8


def _winattn_body(par_ref, q_ref, kp_ref, kc_ref, kn_ref, vp_ref, vc_ref, vn_ref, o_ref, *, seq):
    i = pl.program_id(1)
    kv = pl.program_id(2)
    kall = jnp.concatenate([kp_ref[...], kc_ref[...], kn_ref[...]], axis=0)
    vall = jnp.concatenate([vp_ref[...], vc_ref[...], vn_ref[...]], axis=0)
    qi = lax.broadcasted_iota(jnp.int32, (BLOCK, 3 * BLOCK), 0)
    kj = lax.broadcasted_iota(jnp.int32, (BLOCK, 3 * BLOCK), 1)
    dist = jnp.abs(kj - BLOCK - qi)
    distf = dist.astype(_F32)
    in_window = dist <= WINDOW
    krow = lax.broadcasted_iota(jnp.int32, (1, 3 * BLOCK), 1)
    bias2, sink2 = [], []
    for g in range(A_GROUP):
        slope = par_ref[kv * A_GROUP + g]
        bias2.append(jnp.where(in_window, (-_LOG2E * slope) * distf, -jnp.inf))
        sink2.append(_LOG2E * par_ref[A_HEADS + kv * A_GROUP + g])

    def scores(blk):
        rows = slice(blk * BLOCK, (blk + 1) * BLOCK)
        q = jnp.concatenate([q_ref[rows, g * HEAD_DIM:(g + 1) * HEAD_DIM] for g in range(A_GROUP)], axis=0)
        return lax.dot_general(q, kall[blk * BLOCK:(blk + 3) * BLOCK], _NT, preferred_element_type=_F32)

    def softmax(blk, s):
        kpos = (i * _WIN_QBLOCKS + blk - 1) * BLOCK + krow
        edge2 = jnp.where((kpos >= 0) & (kpos < seq), 0.0, -jnp.inf)
        ps = []
        for g in range(A_GROUP):
            t = s[g * BLOCK:(g + 1) * BLOCK] * (_LOG2E / math.sqrt(HEAD_DIM)) + bias2[g] + edge2
            m = jnp.maximum(jnp.max(t, axis=-1, keepdims=True), sink2[g])
            p = jnp.exp2(t - m)
            den = jnp.sum(p, axis=-1, keepdims=True) + jnp.exp2(sink2[g] - m)
            ps.append((p * (1.0 / den)).astype(_BF16))
        return jnp.concatenate(ps, axis=0)

    def values(blk, p):
        rows = slice(blk * BLOCK, (blk + 1) * BLOCK)
        o = jnp.dot(p, vall[blk * BLOCK:(blk + 3) * BLOCK], preferred_element_type=_F32)
        for g in range(A_GROUP):
            o_ref[rows, g * HEAD_DIM:(g + 1) * HEAD_DIM] = o[g * BLOCK:(g + 1) * BLOCK].astype(o_ref.dtype)

    s_cur = scores(0)
    p_prev = None
    for blk in range(_WIN_QBLOCKS):
        s_next = scores(blk + 1) if blk + 1 < _WIN_QBLOCKS else None
        p_cur = softmax(blk, s_cur)
        if p_prev is not None:
            values(blk - 1, p_prev)
        s_cur, p_prev = s_next, p_cur
    values(_WIN_QBLOCKS - 1, p_prev)


def _winattn(proj, params_a, *, batch, seq):
    t = batch * seq
    nb = seq // BLOCK
    qb = _WIN_QBLOCKS
    assert nb % qb == 0
    nq = nb // qb
    qw = A_GROUP * HEAD_DIM

    def edge_spec(col0, first):
        def imap(b, i, kv):
            n = jnp.clip(i * qb - 1, 0, nb - 1) if first else jnp.clip((i + 1) * qb, 0, nb - 1)
            return (b * nb + n, col0 + kv)
        return pl.BlockSpec((BLOCK, HEAD_DIM), imap)

    def mid_spec(col0):
        return pl.BlockSpec((qb * BLOCK, HEAD_DIM), lambda b, i, kv: (b * nq + i, col0 + kv))

    return pl.pallas_call(
        functools.partial(_winattn_body, seq=seq),
        grid=(batch, nq, A_KV_HEADS),
        in_specs=[
            pl.BlockSpec(memory_space=pltpu.SMEM),
            pl.BlockSpec((qb * BLOCK, qw), lambda b, i, kv: (b * nq + i, kv)),
            edge_spec(_AK_BLK, True), mid_spec(_AK_BLK), edge_spec(_AK_BLK, False),
            edge_spec(_AV_BLK, True), mid_spec(_AV_BLK), edge_spec(_AV_BLK, False),
        ],
        out_specs=pl.BlockSpec((qb * BLOCK, qw), lambda b, i, kv: (b * nq + i, kv)),
        out_shape=jax.ShapeDtypeStruct((t, A_Q_W), _BF16),
        compiler_params=pltpu.CompilerParams(
            dimension_semantics=("parallel", "parallel", "parallel")),
        name="winattn",
    )(params_a, proj, proj, proj, proj, proj, proj, proj)


_ONES_ROWS = 16
_N_SPLIT = 3
_EXP_ZERO = 110.0
_NORM_SLACK = 2.1
_Q_SUBTILES = 4


def _split_bf16(t):
    pieces = []
    for _ in range(_N_SPLIT):
        piece = t.astype(_BF16)
        pieces.append(piece)
        t = t - piece.astype(_F32)
    return pieces


def _diffattn_body(par_ref, q_ref, k_ref, v_ref, lq1_ref, lk1_ref, lq2_ref, lk2_ref, g_ref, o_ref,
                   qm_ref, ka_ref, vt_ref, kn_ref, nb_ref, a0_ref, a1_ref, m_ref, acc_ref, *, tq, tk, seq):
    h = pl.program_id(1)
    qi = pl.program_id(2)
    nk = seq // tk
    slope = par_ref[h]
    inv_slope = par_ref[B_HEADS + h]
    lane_k = lax.broadcasted_iota(jnp.int32, (tk, HEAD_DIM), 1)

    def half_sq_norms_max(n2):
        return jnp.max(jnp.max(n2, axis=0, keepdims=True), axis=1, keepdims=True)

    def max_sq_norm(x):
        sq = x.astype(_F32) ** 2
        lane = lax.broadcasted_iota(jnp.int32, sq.shape, 1)
        s0 = jnp.sum(jnp.where(lane < B_QK_DIM, sq, 0.0), axis=1, keepdims=True)
        s1 = jnp.sum(jnp.where(lane >= B_QK_DIM, sq, 0.0), axis=1, keepdims=True)
        return half_sq_norms_max(jnp.maximum(s0, s1))

    def max_sq_norm_mxu(x):
        sq = (x.astype(_F32) ** 2).astype(_BF16)
        row = lax.broadcasted_iota(jnp.int32, (HEAD_DIM, HEAD_DIM), 0)
        colm = lax.broadcasted_iota(jnp.int32, (HEAD_DIM, HEAD_DIM), 1)
        half_sum = jnp.where((row // B_QK_DIM) == colm, 1.0, 0.0).astype(_BF16)
        return half_sq_norms_max(jnp.dot(sq, half_sum, preferred_element_type=_F32))

    @pl.when(qi == 0)
    def _():
        r = lax.broadcasted_iota(jnp.int32, (tk, HEAD_DIM), 0).astype(_F32)
        feats = []
        for mp in range(2):
            base = (1 - mp) * B_QK_DIM
            f = jnp.zeros((tk, HEAD_DIM), _F32)
            for c, piece in enumerate(_split_bf16(slope * r)):
                f = jnp.where(lane_k == base + c, piece.astype(_F32), f)
            feats.append(f)
        ones = jnp.ones((_ONES_ROWS, tk), _BF16)
        kn2 = jnp.zeros((1, 1), _F32)
        for t in range(nk):
            vt_ref[t] = jnp.concatenate([v_ref[t * tk:(t + 1) * tk, :].T, ones], axis=0)
            k = k_ref[t * tk:(t + 1) * tk, :]
            kn2 = jnp.maximum(kn2, max_sq_norm_mxu(k))
            ka_ref[0, t * tk:(t + 1) * tk, :] = jnp.where(lane_k < B_QK_DIM, k.astype(_F32), feats[0]).astype(_BF16)
            ka_ref[1, t * tk:(t + 1) * tk, :] = jnp.where(lane_k >= B_QK_DIM, k.astype(_F32), feats[1]).astype(_BF16)
        kn_ref[...] = kn2
        if tq == tk:
            krow = lax.broadcasted_iota(jnp.int32, (tk, tq), 0)
            qcol = lax.broadcasted_iota(jnp.int32, (tk, tq), 1)
            nb_ref[...] = slope * jnp.abs(qcol - krow).astype(_F32)

    def query_tile(sub, carry):
        rows = pl.ds(pl.multiple_of(sub * tq, tq), tq)
        q = q_ref[rows, :]
        lane = lax.broadcasted_iota(jnp.int32, (tq, HEAD_DIM), 1)
        first_half = lane < B_QK_DIM
        for mp in range(2):
            own = first_half if mp == 0 else jnp.logical_not(first_half)
            base = (1 - mp) * B_QK_DIM
            sel = jnp.where((lane >= base) & (lane < base + _N_SPLIT), 1.0, 0.0)
            for kind, cols in enumerate((sel, -sel, jnp.zeros_like(sel))):
                qm_ref[kind, mp] = jnp.where(own, q.astype(_F32), cols).astype(_BF16)
        m_ref[...] = jnp.full_like(m_ref, -jnp.inf)
        acc_ref[...] = jnp.zeros_like(acc_ref)

        col = lax.broadcasted_iota(jnp.int32, (1, 2 * tq), 1)
        q0 = (qi * _Q_SUBTILES + sub) * tq
        qpos = (q0 + jnp.where(col < tq, col, col - tq)).astype(_F32)

        def update(a, c, vt):
            m_old = m_ref[...]
            m_new = jnp.maximum(m_old, jnp.max(a, axis=0, keepdims=True) + c)
            alpha = jnp.exp(m_old - m_new)
            p = jnp.exp(a - (m_new - c))
            pv = jnp.dot(vt, p.astype(_BF16), preferred_element_type=_F32)
            acc_ref[...] = alpha * acc_ref[...] + pv
            m_ref[...] = m_new

        kd = q0 // tk
        reach = (_EXP_ZERO + _NORM_SLACK * jnp.sqrt(max_sq_norm(q) * kn_ref[...])) * inv_slope
        reach = jnp.where(reach < seq, reach, float(seq)).astype(jnp.int32)[0, 0] + 1
        lo = jnp.clip((q0 - reach) // tk, 0, kd)
        hi = jnp.clip((q0 + tq - 1 + reach) // tk, kd, nk - 1)
        even = (hi - lo) % 2 == 0
        grow_hi = even & (hi < nk - 1)
        hi = hi + grow_hi.astype(jnp.int32)
        lo = lo - (even & jnp.logical_not(grow_hi)).astype(jnp.int32)
        n_far = hi - lo

        def far_index(t):
            ki = lo + t
            ki = ki + (ki >= kd).astype(jnp.int32)
            return ki, (ki > kd).astype(jnp.int32), pl.multiple_of(ki * tk, tk)

        def scores(k0, kind, a_ref):
            for mp in range(2):
                a_ref[:, mp * tq:(mp + 1) * tq] = lax.dot_general(
                    ka_ref[mp, pl.ds(k0, tk), :], qm_ref[kind, mp], _NT, preferred_element_type=_F32)

        def far_scores(t, a_ref):
            _, after, k0 = far_index(t)
            scores(k0, after, a_ref)

        def far_fold(t, a_ref):
            ki, after, k0 = far_index(t)
            sign = (1 - 2 * after).astype(_F32)
            c = (sign * slope) * (k0.astype(_F32) - qpos)
            update(a_ref[...], c, vt_ref[ki])

        def near_scores(a_ref):
            scores(pl.multiple_of(kd * tk, tk), 2, a_ref)

        def near_fold(a_ref):
            if tq == tk:
                bias = nb_ref[...]
            else:
                krow = lax.broadcasted_iota(jnp.int32, (tk, tq), 0)
                qcol = lax.broadcasted_iota(jnp.int32, (tk, tq), 1)
                bias = slope * jnp.abs(qcol - krow + (q0 - kd * tk)).astype(_F32)
            update(a_ref[...] - jnp.concatenate([bias, bias], axis=1), jnp.zeros((1, 2 * tq), _F32), vt_ref[kd])

        near_scores(a0_ref)
        far_scores(jnp.int32(0), a1_ref)
        near_fold(a0_ref)

        def far_pair(j, carry):
            t = 2 * j
            far_scores(t + 1, a0_ref)
            far_fold(t, a1_ref)
            far_scores(t + 2, a1_ref)
            far_fold(t + 1, a0_ref)
            return carry

        def far_pairs(count):
            def body(j, carry):
                for u in range(count):
                    carry = far_pair(count * j + u, carry)
                return carry
            return body

        n_pairs = (n_far - 1) // 2
        done = jnp.int32(0)
        for count in (4, 2, 1):
            trips = (n_pairs - done) // count
            lax.fori_loop(done // count, done // count + trips, far_pairs(count), 0)
            done = done + trips * count
        far_fold(n_far - 1, a1_ref)

        lam = (jnp.exp(jnp.sum(lq1_ref[...] * lk1_ref[...], axis=-1, keepdims=True))
               - jnp.exp(jnp.sum(lq2_ref[...] * lk2_ref[...], axis=-1, keepdims=True))
               + LAMBDA_INIT)
        acc = acc_ref[...]
        on = acc[0:B_V_DIM] / acc[B_V_DIM:B_V_DIM + 1]
        o = on[:, 0:tq] - lam * on[:, tq:2 * tq]
        y = o * lax.rsqrt(jnp.mean(o * o, axis=0, keepdims=True) + EPS)
        o_ref[rows, :] = ((y.T * g_ref[...]) * (1.0 - LAMBDA_INIT)).astype(o_ref.dtype)
        return carry

    lax.fori_loop(0, _Q_SUBTILES, query_tile, 0)


def _diffattn(proj, params_b, lq1, lk1, lq2, lk2, subln_g, *, batch, seq, tq=512, tk=512):
    t = batch * seq
    tqs = tq * _Q_SUBTILES
    nq = seq // tqs
    nk = seq // tk
    assert tk % tq == 0 and seq % tk == 0 and nk % 2 == 0 and nk >= 2 and seq % tqs == 0
    small = lambda shape: pl.BlockSpec(shape, lambda b, h, i: (0, 0))
    return pl.pallas_call(
        functools.partial(_diffattn_body, tq=tq, tk=tk, seq=seq),
        grid=(batch, B_HEADS, nq),
        in_specs=[
            pl.BlockSpec(memory_space=pltpu.SMEM),
            pl.BlockSpec((tqs, HEAD_DIM), lambda b, h, i: (b * nq + i, _BQ_BLK + h)),
            pl.BlockSpec((seq, HEAD_DIM), lambda b, h, i: (b, _BK_BLK + h)),
            pl.BlockSpec((seq, HEAD_DIM), lambda b, h, i: (b, _BV_BLK + h)),
            small((1, B_QK_DIM)), small((1, B_QK_DIM)), small((1, B_QK_DIM)), small((1, B_QK_DIM)),
            small((1, B_V_DIM)),
        ],
        out_specs=pl.BlockSpec((tqs, B_V_DIM), lambda b, h, i: (b * nq + i, h)),
        out_shape=jax.ShapeDtypeStruct((t, B_V_W), _BF16),
        scratch_shapes=[
            pltpu.VMEM((3, 2, tq, HEAD_DIM), _BF16),
            pltpu.VMEM((2, seq, HEAD_DIM), _BF16),
            pltpu.VMEM((nk, B_V_DIM + _ONES_ROWS, tk), _BF16),
            pltpu.VMEM((1, 1), _F32),
            pltpu.VMEM((tk, tq) if tq == tk else (8, 128), _F32),
            pltpu.VMEM((tk, 2 * tq), _F32),
            pltpu.VMEM((tk, 2 * tq), _F32),
            pltpu.VMEM((1, 2 * tq), _F32),
            pltpu.VMEM((B_V_DIM + _ONES_ROWS, 2 * tq), _F32),
        ],
        compiler_params=pltpu.CompilerParams(
            dimension_semantics=("parallel", "parallel", "arbitrary"),
            vmem_limit_bytes=_VMEM_LIMIT_BYTES),
        name="diffattn",
    )(params_b, proj, proj, proj, lq1, lk1, lq2, lk2, subln_g)


def _outproj_body(oa_ref, ob_ref, w_ref, x_ref, o_ref):
    a = jnp.concatenate([oa_ref[...], ob_ref[...]], axis=1)
    o_ref[...] = x_ref[...] + jnp.dot(a, w_ref[...], preferred_element_type=_F32)


def _outproj(oa, ob, w, x2d, *, tm=512):
    t = x2d.shape[0]
    return pl.pallas_call(
        _outproj_body,
        grid=(t // tm,),
        in_specs=[
            pl.BlockSpec((tm, A_Q_W), lambda i: (i, 0)),
            pl.BlockSpec((tm, B_V_W), lambda i: (i, 0)),
            pl.BlockSpec((MIX_WIDTH, D_MODEL), lambda i: (0, 0)),
            pl.BlockSpec((tm, D_MODEL), lambda i: (i, 0)),
        ],
        out_specs=pl.BlockSpec((tm, D_MODEL), lambda i: (i, 0)),
        out_shape=jax.ShapeDtypeStruct((t, D_MODEL), _F32),
        compiler_params=pltpu.CompilerParams(
            dimension_semantics=("parallel",),
            vmem_limit_bytes=_VMEM_LIMIT_BYTES),
        name="outproj",
    )(oa, ob, w, x2d)


def _mlp_body(x_ref, gm_ref, wu_ref, wd_ref, gf_ref, o_ref, h_ref, acc_ref):
    j = pl.program_id(1)

    @pl.when(j == 0)
    def _():
        x = x_ref[...]
        h_ref[...] = (_rms_scale(x) * gm_ref[...]).astype(_BF16)
        acc_ref[...] = x

    u = jnp.maximum(jnp.dot(h_ref[...], wu_ref[...], preferred_element_type=_F32), 0.0)
    acc_ref[...] += jnp.dot((u * u).astype(_BF16), wd_ref[...], preferred_element_type=_F32)

    @pl.when(j == pl.num_programs(1) - 1)
    def _():
        o_ref[...] = _rms_scale(acc_ref[...]) * gf_ref[...]


def _mlp(x1, g_mlp, w_up, w_down, g_final, *, tm=512, tf=1024):
    t = x1.shape[0]
    return pl.pallas_call(
        _mlp_body,
        grid=(t // tm, D_FF // tf),
        in_specs=[
            pl.BlockSpec((tm, D_MODEL), lambda i, j: (i, 0)),
            pl.BlockSpec((1, D_MODEL), lambda i, j: (0, 0)),
            pl.BlockSpec((D_MODEL, tf), lambda i, j: (0, j)),
            pl.BlockSpec((tf, D_MODEL), lambda i, j: (j, 0)),
            pl.BlockSpec((1, D_MODEL), lambda i, j: (0, 0)),
        ],
        out_specs=pl.BlockSpec((tm, D_MODEL), lambda i, j: (i, 0)),
        out_shape=jax.ShapeDtypeStruct((t, D_MODEL), _F32),
        scratch_shapes=[pltpu.VMEM((tm, D_MODEL), _BF16), pltpu.VMEM((tm, D_MODEL), _F32)],
        compiler_params=pltpu.CompilerParams(
            dimension_semantics=("parallel", "arbitrary"),
            vmem_limit_bytes=_VMEM_LIMIT_BYTES),
        name="mlp",
    )(x1, g_mlp, w_up, w_down, g_final)


def _alibi_slopes():
    i = jnp.arange(1, N_HEADS_TOTAL + 1, dtype=_F32)
    s = jnp.exp2(-8.0 / N_HEADS_TOTAL * i)
    return s[0::2], s[1::2]


def _encoder(x, p):
    batch, seq = x.shape[0], x.shape[1]
    x2d = x.reshape(batch * seq, D_MODEL)
    proj = _inproj(x2d, p["g_attn"], p["w_in"], p["colscale"])
    oa = _winattn(proj, p["params_a"], batch=batch, seq=seq)
    ob = _diffattn(proj, p["params_b"], p["lq1"], p["lk1"], p["lq2"], p["lk2"], p["subln_g"],
                   batch=batch, seq=seq)
    x1 = _outproj(oa, ob, p["w_out"], x2d)
    y = _mlp(x1, p["g_mlp"], p["w_up"], p["w_down"], p["g_final"])
    return y.reshape(batch, seq, D_MODEL)


def kernel(x_prompt, x_sample, norm_attn_g, w_in, sink_logits, lambda_q1, lambda_k1, lambda_q2, lambda_k2,
           diff_subln_g, w_out, norm_mlp_g, w_up, w_down, norm_final_g):
    slopes_a, slopes_b = _alibi_slopes()
    col = jnp.arange(IN_WIDTH)
    in_bq = (col >= _BQ_BLK * HEAD_DIM) & (col < _BK_BLK * HEAD_DIM)
    p = {
        "g_attn": norm_attn_g[0].reshape(1, D_MODEL).astype(_F32),
        "w_in": w_in[0].astype(_BF16),
        "colscale": jnp.where(in_bq, 1.0 / math.sqrt(B_QK_DIM), 1.0).astype(_F32).reshape(1, IN_WIDTH),
        "params_a": jnp.concatenate([slopes_a, sink_logits[0].astype(_F32)]),
        "params_b": jnp.concatenate([slopes_b, 1.0 / slopes_b]),
        "lq1": lambda_q1[0].reshape(1, B_QK_DIM).astype(_F32),
        "lk1": lambda_k1[0].reshape(1, B_QK_DIM).astype(_F32),
        "lq2": lambda_q2[0].reshape(1, B_QK_DIM).astype(_F32),
        "lk2": lambda_k2[0].reshape(1, B_QK_DIM).astype(_F32),
        "subln_g": diff_subln_g[0].reshape(1, B_V_DIM).astype(_F32),
        "w_out": w_out[0].astype(_BF16),
        "g_mlp": norm_mlp_g[0].reshape(1, D_MODEL).astype(_F32),
        "w_up": w_up[0].astype(_BF16),
        "w_down": w_down[0].astype(_BF16),
        "g_final": norm_final_g.reshape(1, D_MODEL).astype(_F32),
    }
    return (_encoder(x_prompt, p), _encoder(x_sample, p))
```

```python
import functools
import math

import jax
import jax.numpy as jnp
from jax import lax
from jax.experimental import pallas as pl
from jax.experimental.pallas import tpu as pltpu

D_MODEL = 2048
HEAD_DIM = 128
N_HEADS_TOTAL = D_MODEL // HEAD_DIM
A_HEADS = N_HEADS_TOTAL // 2
A_KV_HEADS = 2
A_GROUP = A_HEADS // A_KV_HEADS
WINDOW = 128
BLOCK = 128
B_HEADS = N_HEADS_TOTAL - A_HEADS
B_QK_DIM = HEAD_DIM // 2
B_V_DIM = HEAD_DIM
MIX_WIDTH = A_HEADS * HEAD_DIM + B_HEADS * B_V_DIM
D_FF = 4 * D_MODEL
EPS = 1e-5
A_Q_W = A_HEADS * HEAD_DIM
A_KV_W = A_KV_HEADS * HEAD_DIM
B_QK_W = B_HEADS * 2 * B_QK_DIM
B_V_W = B_HEADS * B_V_DIM
IN_WIDTH = A_Q_W + 2 * A_KV_W + 2 * B_QK_W + B_V_W
LAMBDA_INIT = 0.8 - 0.6 * math.exp(-0.3 * 0)

_AK_BLK = A_Q_W // HEAD_DIM
_AV_BLK = (A_Q_W + A_KV_W) // HEAD_DIM
_BQ_BLK = (A_Q_W + 2 * A_KV_W) // HEAD_DIM
_BK_BLK = _BQ_BLK + B_QK_W // HEAD_DIM
_BV_BLK = _BK_BLK + B_QK_W // HEAD_DIM

_VMEM_LIMIT_BYTES = 56 * 1024 * 1024

_NT = (((1,), (1,)), ((), ()))

_BF16 = jnp.bfloat16
_F32 = jnp.float32
_LOG2E = math.log2(math.e)


def _rms_scale(x):
    return x * lax.rsqrt(jnp.mean(x * x, axis=-1, keepdims=True) + EPS)


def _inproj_body(x_ref, g_ref, w_ref, cs_ref, o_ref, h_ref):
    @pl.when(pl.program_id(1) == 0)
    def _():
        h_ref[...] = (_rms_scale(x_ref[...]) * g_ref[...]).astype(_BF16)

    acc = jnp.dot(h_ref[...], w_ref[...], preferred_element_type=_F32)
    o_ref[...] = (acc * cs_ref[...]).astype(o_ref.dtype)


def _inproj(x2d, g, w, colscale, *, tm=1024, tn=1536):
    t = x2d.shape[0]
    return pl.pallas_call(
        _inproj_body,
        grid=(t // tm, IN_WIDTH // tn),
        in_specs=[
            pl.BlockSpec((tm, D_MODEL), lambda i, j: (i, 0)),
            pl.BlockSpec((1, D_MODEL), lambda i, j: (0, 0)),
            pl.BlockSpec((D_MODEL, tn), lambda i, j: (0, j)),
            pl.BlockSpec((1, tn), lambda i, j: (0, j)),
        ],
        out_specs=pl.BlockSpec((tm, tn), lambda i, j: (i, j)),
        out_shape=jax.ShapeDtypeStruct((t, IN_WIDTH), _BF16),
        scratch_shapes=[pltpu.VMEM((tm, D_MODEL), _BF16)],
        compiler_params=pltpu.CompilerParams(
            dimension_semantics=("parallel", "arbitrary"),
            vmem_limit_bytes=_VMEM_LIMIT_BYTES),
        name="inproj",
    )(x2d, g, w, colscale)


_WIN_QBLOCKS = 8


def _winattn_body(par_ref, q_ref, kp_ref, kc_ref, kn_ref, vp_ref, vc_ref, vn_ref, o_ref, *, seq):
    i = pl.program_id(1)
    kv = pl.program_id(2)
    kall = jnp.concatenate([kp_ref[...], kc_ref[...], kn_ref[...]], axis=0)
    vall = jnp.concatenate([vp_ref[...], vc_ref[...], vn_ref[...]], axis=0)
    qi = lax.broadcasted_iota(jnp.int32, (BLOCK, 3 * BLOCK), 0)
    kj = lax.broadcasted_iota(jnp.int32, (BLOCK, 3 * BLOCK), 1)
    dist = jnp.abs(kj - BLOCK - qi)
    distf = dist.astype(_F32)
    in_window = dist <= WINDOW
    krow = lax.broadcasted_iota(jnp.int32, (1, 3 * BLOCK), 1)
    bias2, sink2 = [], []
    for g in range(A_GROUP):
        slope = par_ref[kv * A_GROUP + g]
        bias2.append(jnp.where(in_window, (-_LOG2E * slope) * distf, -jnp.inf))
        sink2.append(_LOG2E * par_ref[A_HEADS + kv * A_GROUP + g])

    def scores(blk):
        rows = slice(blk * BLOCK, (blk + 1) * BLOCK)
        q = jnp.concatenate([q_ref[rows, g * HEAD_DIM:(g + 1) * HEAD_DIM] for g in range(A_GROUP)], axis=0)
        return lax.dot_general(q, kall[blk * BLOCK:(blk + 3) * BLOCK], _NT, preferred_element_type=_F32)

    def softmax(blk, s):
        kpos = (i * _WIN_QBLOCKS + blk - 1) * BLOCK + krow
        edge2 = jnp.where((kpos >= 0) & (kpos < seq), 0.0, -jnp.inf)
        ps = []
        for g in range(A_GROUP):
            t = s[g * BLOCK:(g + 1) * BLOCK] * (_LOG2E / math.sqrt(HEAD_DIM)) + bias2[g] + edge2
            m = jnp.maximum(jnp.max(t, axis=-1, keepdims=True), sink2[g])
            p = jnp.exp2(t - m)
            den = jnp.sum(p, axis=-1, keepdims=True) + jnp.exp2(sink2[g] - m)
            ps.append((p * (1.0 / den)).astype(_BF16))
        return jnp.concatenate(ps, axis=0)

    def values(blk, p):
        rows = slice(blk * BLOCK, (blk + 1) * BLOCK)
        o = jnp.dot(p, vall[blk * BLOCK:(blk + 3) * BLOCK], preferred_element_type=_F32)
        for g in range(A_GROUP):
            o_ref[rows, g * HEAD_DIM:(g + 1) * HEAD_DIM] = o[g * BLOCK:(g + 1) * BLOCK].astype(o_ref.dtype)

    s_cur = scores(0)
    p_prev = None
    for blk in range(_WIN_QBLOCKS):
        s_next = scores(blk + 1) if blk + 1 < _WIN_QBLOCKS else None
        p_cur = softmax(blk, s_cur)
        if p_prev is not None:
            values(blk - 1, p_prev)
        s_cur, p_prev = s_next, p_cur
    values(_WIN_QBLOCKS - 1, p_prev)


def _winattn(proj, params_a, *, batch, seq):
    t = batch * seq
    nb = seq // BLOCK
    qb = _WIN_QBLOCKS
    assert nb % qb == 0
    nq = nb // qb
    qw = A_GROUP * HEAD_DIM

    def edge_spec(col0, first):
        def imap(b, i, kv):
            n = jnp.clip(i * qb - 1, 0, nb - 1) if first else jnp.clip((i + 1) * qb, 0, nb - 1)
            return (b * nb + n, col0 + kv)
        return pl.BlockSpec((BLOCK, HEAD_DIM), imap)

    def mid_spec(col0):
        return pl.BlockSpec((qb * BLOCK, HEAD_DIM), lambda b, i, kv: (b * nq + i, col0 + kv))

    return pl.pallas_call(
        functools.partial(_winattn_body, seq=seq),
        grid=(batch, nq, A_KV_HEADS),
        in_specs=[
            pl.BlockSpec(memory_space=pltpu.SMEM),
            pl.BlockSpec((qb * BLOCK, qw), lambda b, i, kv: (b * nq + i, kv)),
            edge_spec(_AK_BLK, True), mid_spec(_AK_BLK), edge_spec(_AK_BLK, False),
            edge_spec(_AV_BLK, True), mid_spec(_AV_BLK), edge_spec(_AV_BLK, False),
        ],
        out_specs=pl.BlockSpec((qb * BLOCK, qw), lambda b, i, kv: (b * nq + i, kv)),
        out_shape=jax.ShapeDtypeStruct((t, A_Q_W), _BF16),
        compiler_params=pltpu.CompilerParams(
            dimension_semantics=("parallel", "parallel", "parallel")),
        name="winattn",
    )(params_a, proj, proj, proj, proj, proj, proj, proj)


_ONES_ROWS = 16
_N_SPLIT = 3
_EXP_ZERO = 110.0
_NORM_SLACK = 2.1
_Q_SUBTILES = 4
_FIXED_MAX_SPAN = 60.0


def _split_bf16(t):
    pieces = []
    for _ in range(_N_SPLIT):
        piece = t.astype(_BF16)
        pieces.append(piece)
        t = t - piece.astype(_F32)
    return pieces


def _diffattn_body(par_ref, q_ref, k_ref, v_ref, lq1_ref, lk1_ref, lq2_ref, lk2_ref, g_ref, o_ref,
                   qm_ref, ka_ref, vt_ref, kn_ref, nb_ref, a0_ref, a1_ref, p0_ref, p1_ref, m_ref, acc_ref, *, tq, tk, seq):
    h = pl.program_id(1)
    qi = pl.program_id(2)
    nk = seq // tk
    slope = par_ref[h]
    inv_slope = par_ref[B_HEADS + h]
    lane_k = lax.broadcasted_iota(jnp.int32, (tk, HEAD_DIM), 1)

    def half_sq_norms_max(n2):
        return jnp.max(jnp.max(n2, axis=0, keepdims=True), axis=1, keepdims=True)

    def max_sq_norm(x):
        sq = x.astype(_F32) ** 2
        lane = lax.broadcasted_iota(jnp.int32, sq.shape, 1)
        s0 = jnp.sum(jnp.where(lane < B_QK_DIM, sq, 0.0), axis=1, keepdims=True)
        s1 = jnp.sum(jnp.where(lane >= B_QK_DIM, sq, 0.0), axis=1, keepdims=True)
        return half_sq_norms_max(jnp.maximum(s0, s1))

    def max_sq_norm_mxu(x):
        sq = (x.astype(_F32) ** 2).astype(_BF16)
        row = lax.broadcasted_iota(jnp.int32, (HEAD_DIM, HEAD_DIM), 0)
        colm = lax.broadcasted_iota(jnp.int32, (HEAD_DIM, HEAD_DIM), 1)
        half_sum = jnp.where((row // B_QK_DIM) == colm, 1.0, 0.0).astype(_BF16)
        return half_sq_norms_max(jnp.dot(sq, half_sum, preferred_element_type=_F32))

    @pl.when(qi == 0)
    def _():
        r = lax.broadcasted_iota(jnp.int32, (tk, HEAD_DIM), 0).astype(_F32)
        feats = []
        for mp in range(2):
            base = (1 - mp) * B_QK_DIM
            f = jnp.zeros((tk, HEAD_DIM), _F32)
            for c, piece in enumerate(_split_bf16(slope * r)):
                f = jnp.where(lane_k == base + c, piece.astype(_F32), f)
            feats.append(f)
        ones = jnp.ones((_ONES_ROWS, tk), _BF16)
        kn2 = jnp.zeros((1, 1), _F32)
        for t in range(nk):
            vt_ref[t] = jnp.concatenate([v_ref[t * tk:(t + 1) * tk, :].T, ones], axis=0)
            k = k_ref[t * tk:(t + 1) * tk, :]
            kn2 = jnp.maximum(kn2, max_sq_norm_mxu(k))
            ka_ref[0, t * tk:(t + 1) * tk, :] = jnp.where(lane_k < B_QK_DIM, k.astype(_F32), feats[0]).astype(_BF16)
            ka_ref[1, t * tk:(t + 1) * tk, :] = jnp.where(lane_k >= B_QK_DIM, k.astype(_F32), feats[1]).astype(_BF16)
        kn_ref[...] = kn2
        if tq == tk:
            krow = lax.broadcasted_iota(jnp.int32, (tk, tq), 0)
            qcol = lax.broadcasted_iota(jnp.int32, (tk, tq), 1)
            nb_ref[...] = slope * jnp.abs(qcol - krow).astype(_F32)

    def query_tile(sub, carry):
        rows = pl.ds(pl.multiple_of(sub * tq, tq), tq)
        q = q_ref[rows, :]
        lane = lax.broadcasted_iota(jnp.int32, (tq, HEAD_DIM), 1)
        first_half = lane < B_QK_DIM
        for mp in range(2):
            own = first_half if mp == 0 else jnp.logical_not(first_half)
            base = (1 - mp) * B_QK_DIM
            sel = jnp.where((lane >= base) & (lane < base + _N_SPLIT), 1.0, 0.0)
            for kind, cols in enumerate((sel, -sel, jnp.zeros_like(sel))):
                qm_ref[kind, mp] = jnp.where(own, q.astype(_F32), cols).astype(_BF16)
        m_ref[...] = jnp.full_like(m_ref, -jnp.inf)
        acc_ref[...] = jnp.zeros_like(acc_ref)

        col = lax.broadcasted_iota(jnp.int32, (1, 2 * tq), 1)
        q0 = (qi * _Q_SUBTILES + sub) * tq
        qpos = (q0 + jnp.where(col < tq, col, col - tq)).astype(_F32)

        def update(a, c, vt):
            m_old = m_ref[...]
            m_new = jnp.maximum(m_old, jnp.max(a, axis=0, keepdims=True) + c)
            alpha = jnp.exp(m_old - m_new)
            p = jnp.exp(a - (m_new - c))
            pv = jnp.dot(vt, p.astype(_BF16), preferred_element_type=_F32)
            acc_ref[...] = alpha * acc_ref[...] + pv
            m_ref[...] = m_new

        kd = q0 // tk
        span = _NORM_SLACK * jnp.sqrt(max_sq_norm(q) * kn_ref[...])
        reach = (_EXP_ZERO + span) * inv_slope
        reach = jnp.where(reach < seq, reach, float(seq)).astype(jnp.int32)[0, 0] + 1
        lo = jnp.clip((q0 - reach) // tk, 0, kd)
        hi = jnp.clip((q0 + tq - 1 + reach) // tk, kd, nk - 1)
        even = (hi - lo) % 2 == 0
        grow_hi = even & (hi < nk - 1)
        hi = hi + grow_hi.astype(jnp.int32)
        lo = lo - (even & jnp.logical_not(grow_hi)).astype(jnp.int32)
        n_far = hi - lo

        def far_index(t):
            ki = lo + t
            ki = ki + (ki >= kd).astype(jnp.int32)
            return ki, (ki > kd).astype(jnp.int32), pl.multiple_of(ki * tk, tk)

        def scores(k0, kind, a_ref):
            for mp in range(2):
                a_ref[:, mp * tq:(mp + 1) * tq] = lax.dot_general(
                    ka_ref[mp, pl.ds(k0, tk), :], qm_ref[kind, mp], _NT, preferred_element_type=_F32)

        def far_scores(t, a_ref):
            _, after, k0 = far_index(t)
            scores(k0, after, a_ref)

        def far_fold(t, a_ref):
            ki, after, k0 = far_index(t)
            sign = (1 - 2 * after).astype(_F32)
            c = (sign * slope) * (k0.astype(_F32) - qpos)
            update(a_ref[...], c, vt_ref[ki])

        def near_scores(a_ref):
            scores(pl.multiple_of(kd * tk, tk), 2, a_ref)

        def near_fold(a_ref):
            if tq == tk:
                bias = nb_ref[...]
            else:
                krow = lax.broadcasted_iota(jnp.int32, (tk, tq), 0)
                qcol = lax.broadcasted_iota(jnp.int32, (tk, tq), 1)
                bias = slope * jnp.abs(qcol - krow + (q0 - kd * tk)).astype(_F32)
            update(a_ref[...] - jnp.concatenate([bias, bias], axis=1), jnp.zeros((1, 2 * tq), _F32), vt_ref[kd])

        def far_shift(t):
            ki, after, k0 = far_index(t)
            sign = (1 - 2 * after).astype(_F32)
            return m_ref[...] - (sign * slope) * (k0.astype(_F32) - qpos)

        def far_probs(t, p_ref):
            _, after, k0 = far_index(t)
            shift = far_shift(t)
            for mp in range(2):
                cols = slice(mp * tq, (mp + 1) * tq)
                a = lax.dot_general(ka_ref[mp, pl.ds(k0, tk), :], qm_ref[after, mp], _NT,
                                    preferred_element_type=_F32)
                p_ref[:, cols] = jnp.exp(a - shift[:, cols]).astype(_BF16)

        def far_values(t, p_ref):
            ki, _, _ = far_index(t)
            acc_ref[...] += jnp.dot(vt_ref[ki], p_ref[...], preferred_element_type=_F32)

        def pair_loops(pair):
            def grouped(count):
                def body(j, carry):
                    for u in range(count):
                        pair(count * j + u)
                    return carry
                return body

            n_pairs = (n_far - 1) // 2
            done = jnp.int32(0)
            for count in (4, 2, 1):
                trips = (n_pairs - done) // count
                lax.fori_loop(done // count, done // count + trips, grouped(count), 0)
                done = done + trips * count

        near_scores(a0_ref)
        far_scores(jnp.int32(0), a1_ref)
        near_fold(a0_ref)

        fixed_reference = jnp.where(span <= _FIXED_MAX_SPAN, 1, 0)[0, 0] == 1

        @pl.when(fixed_reference)
        def _():
            m_ref[...] = m_ref[...] + span
            acc_ref[...] = acc_ref[...] * jnp.exp(-span)
            p0_ref[...] = jnp.exp(a1_ref[...] - far_shift(jnp.int32(0))).astype(_BF16)

            def pair(j):
                t = 2 * j
                far_probs(t + 1, p1_ref)
                far_values(t, p0_ref)
                far_probs(t + 2, p0_ref)
                far_values(t + 1, p1_ref)

            pair_loops(pair)
            far_values(n_far - 1, p0_ref)

        @pl.when(jnp.logical_not(fixed_reference))
        def _():
            def pair(j):
                t = 2 * j
                far_scores(t + 1, a0_ref)
                far_fold(t, a1_ref)
                far_scores(t + 2, a1_ref)
                far_fold(t + 1, a0_ref)

            pair_loops(pair)
            far_fold(n_far - 1, a1_ref)

        lam = (jnp.exp(jnp.sum(lq1_ref[...] * lk1_ref[...], axis=-1, keepdims=True))
               - jnp.exp(jnp.sum(lq2_ref[...] * lk2_ref[...], axis=-1, keepdims=True))
               + LAMBDA_INIT)
        acc = acc_ref[...]
        on = acc[0:B_V_DIM] / acc[B_V_DIM:B_V_DIM + 1]
        o = on[:, 0:tq] - lam * on[:, tq:2 * tq]
        y = o * lax.rsqrt(jnp.mean(o * o, axis=0, keepdims=True) + EPS)
        o_ref[rows, :] = ((y.T * g_ref[...]) * (1.0 - LAMBDA_INIT)).astype(o_ref.dtype)
        return carry

    lax.fori_loop(0, _Q_SUBTILES, query_tile, 0)


def _diffattn(proj, params_b, lq1, lk1, lq2, lk2, subln_g, *, batch, seq, tq=512, tk=512):
    t = batch * seq
    tqs = tq * _Q_SUBTILES
    nq = seq // tqs
    nk = seq // tk
    assert tk % tq == 0 and seq % tk == 0 and nk % 2 == 0 and nk >= 2 and seq % tqs == 0
    small = lambda shape: pl.BlockSpec(shape, lambda b, h, i: (0, 0))
    return pl.pallas_call(
        functools.partial(_diffattn_body, tq=tq, tk=tk, seq=seq),
        grid=(batch, B_HEADS, nq),
        in_specs=[
            pl.BlockSpec(memory_space=pltpu.SMEM),
            pl.BlockSpec((tqs, HEAD_DIM), lambda b, h, i: (b * nq + i, _BQ_BLK + h)),
            pl.BlockSpec((seq, HEAD_DIM), lambda b, h, i: (b, _BK_BLK + h)),
            pl.BlockSpec((seq, HEAD_DIM), lambda b, h, i: (b, _BV_BLK + h)),
            small((1, B_QK_DIM)), small((1, B_QK_DIM)), small((1, B_QK_DIM)), small((1, B_QK_DIM)),
            small((1, B_V_DIM)),
        ],
        out_specs=pl.BlockSpec((tqs, B_V_DIM), lambda b, h, i: (b * nq + i, h)),
        out_shape=jax.ShapeDtypeStruct((t, B_V_W), _BF16),
        scratch_shapes=[
            pltpu.VMEM((3, 2, tq, HEAD_DIM), _BF16),
            pltpu.VMEM((2, seq, HEAD_DIM), _BF16),
            pltpu.VMEM((nk, B_V_DIM + _ONES_ROWS, tk), _BF16),
            pltpu.VMEM((1, 1), _F32),
            pltpu.VMEM((tk, tq) if tq == tk else (8, 128), _F32),
            pltpu.VMEM((tk, 2 * tq), _F32),
            pltpu.VMEM((tk, 2 * tq), _F32),
            pltpu.VMEM((tk, 2 * tq), _BF16),
            pltpu.VMEM((tk, 2 * tq), _BF16),
            pltpu.VMEM((1, 2 * tq), _F32),
            pltpu.VMEM((B_V_DIM + _ONES_ROWS, 2 * tq), _F32),
        ],
        compiler_params=pltpu.CompilerParams(
            dimension_semantics=("parallel", "parallel", "arbitrary"),
            vmem_limit_bytes=_VMEM_LIMIT_BYTES),
        name="diffattn",
    )(params_b, proj, proj, proj, lq1, lk1, lq2, lk2, subln_g)


def _outproj_body(oa_ref, ob_ref, w_ref, x_ref, o_ref):
    a = jnp.concatenate([oa_ref[...], ob_ref[...]], axis=1)
    o_ref[...] = x_ref[...] + jnp.dot(a, w_ref[...], preferred_element_type=_F32)


def _outproj(oa, ob, w, x2d, *, tm=512):
    t = x2d.shape[0]
    return pl.pallas_call(
        _outproj_body,
        grid=(t // tm,),
        in_specs=[
            pl.BlockSpec((tm, A_Q_W), lambda i: (i, 0)),
            pl.BlockSpec((tm, B_V_W), lambda i: (i, 0)),
            pl.BlockSpec((MIX_WIDTH, D_MODEL), lambda i: (0, 0)),
            pl.BlockSpec((tm, D_MODEL), lambda i: (i, 0)),
        ],
        out_specs=pl.BlockSpec((tm, D_MODEL), lambda i: (i, 0)),
        out_shape=jax.ShapeDtypeStruct((t, D_MODEL), _F32),
        compiler_params=pltpu.CompilerParams(
            dimension_semantics=("parallel",),
            vmem_limit_bytes=_VMEM_LIMIT_BYTES),
        name="outproj",
    )(oa, ob, w, x2d)


def _mlp_body(x_ref, gm_ref, wu_ref, wd_ref, gf_ref, o_ref, h_ref, acc_ref):
    j = pl.program_id(1)

    @pl.when(j == 0)
    def _():
        x = x_ref[...]
        h_ref[...] = (_rms_scale(x) * gm_ref[...]).astype(_BF16)
        acc_ref[...] = x

    u = jnp.maximum(jnp.dot(h_ref[...], wu_ref[...], preferred_element_type=_F32), 0.0)
    acc_ref[...] += jnp.dot((u * u).astype(_BF16), wd_ref[...], preferred_element_type=_F32)

    @pl.when(j == pl.num_programs(1) - 1)
    def _():
        o_ref[...] = _rms_scale(acc_ref[...]) * gf_ref[...]


def _mlp(x1, g_mlp, w_up, w_down, g_final, *, tm=512, tf=1024):
    t = x1.shape[0]
    return pl.pallas_call(
        _mlp_body,
        grid=(t // tm, D_FF // tf),
        in_specs=[
            pl.BlockSpec((tm, D_MODEL), lambda i, j: (i, 0)),
            pl.BlockSpec((1, D_MODEL), lambda i, j: (0, 0)),
            pl.BlockSpec((D_MODEL, tf), lambda i, j: (0, j)),
            pl.BlockSpec((tf, D_MODEL), lambda i, j: (j, 0)),
            pl.BlockSpec((1, D_MODEL), lambda i, j: (0, 0)),
        ],
        out_specs=pl.BlockSpec((tm, D_MODEL), lambda i, j: (i, 0)),
        out_shape=jax.ShapeDtypeStruct((t, D_MODEL), _F32),
        scratch_shapes=[pltpu.VMEM((tm, D_MODEL), _BF16), pltpu.VMEM((tm, D_MODEL), _F32)],
        compiler_params=pltpu.CompilerParams(
            dimension_semantics=("parallel", "arbitrary"),
            vmem_limit_bytes=_VMEM_LIMIT_BYTES),
        name="mlp",
    )(x1, g_mlp, w_up, w_down, g_final)


def _alibi_slopes():
    i = jnp.arange(1, N_HEADS_TOTAL + 1, dtype=_F32)
    s = jnp.exp2(-8.0 / N_HEADS_TOTAL * i)
    return s[0::2], s[1::2]


def _encoder(x, p):
    batch, seq = x.shape[0], x.shape[1]
    x2d = x.reshape(batch * seq, D_MODEL)
    proj = _inproj(x2d, p["g_attn"], p["w_in"], p["colscale"])
    oa = _winattn(proj, p["params_a"], batch=batch, seq=seq)
    ob = _diffattn(proj, p["params_b"], p["lq1"], p["lk1"], p["lq2"], p["lk2"], p["subln_g"],
                   batch=batch, seq=seq)
    x1 = _outproj(oa, ob, p["w_out"], x2d)
    y = _mlp(x1, p["g_mlp"], p["w_up"], p["w_down"], p["g_final"])
    return y.reshape(batch, seq, D_MODEL)


def kernel(x_prompt, x_sample, norm_attn_g, w_in, sink_logits, lambda_q1, lambda_k1, lambda_q2, lambda_k2,
           diff_subln_g, w_out, norm_mlp_g, w_up, w_down, norm_final_g):
    slopes_a, slopes_b = _alibi_slopes()
    col = jnp.arange(IN_WIDTH)
    in_bq = (col >= _BQ_BLK * HEAD_DIM) & (col < _BK_BLK * HEAD_DIM)
    p = {
        "g_attn": norm_attn_g[0].reshape(1, D_MODEL).astype(_F32),
        "w_in": w_in[0].astype(_BF16),
        "colscale": jnp.where(in_bq, 1.0 / math.sqrt(B_QK_DIM), 1.0).astype(_F32).reshape(1, IN_WIDTH),
        "params_a": jnp.concatenate([slopes_a, sink_logits[0].astype(_F32)]),
        "params_b": jnp.concatenate([slopes_b, 1.0 / slopes_b]),
        "lq1": lambda_q1[0].reshape(1, B_QK_DIM).astype(_F32),
        "lk1": lambda_k1[0].reshape(1, B_QK_DIM).astype(_F32),
        "lq2": lambda_q2[0].reshape(1, B_QK_DIM).astype(_F32),
        "lk2": lambda_k2[0].reshape(1, B_QK_DIM).astype(_F32),
        "subln_g": diff_subln_g[0].reshape(1, B_V_DIM).astype(_F32),
        "w_out": w_out[0].astype(_BF16),
        "g_mlp": norm_mlp_g[0].reshape(1, D_MODEL).astype(_F32),
        "w_up": w_up[0].astype(_BF16),
        "w_down": w_down[0].astype(_BF16),
        "g_final": norm_final_g.reshape(1, D_MODEL).astype(_F32),
    }
    return (_encoder(x_prompt, p), _encoder(x_sample, p))
```

```python
import functools
import math

import jax
import jax.numpy as jnp
from jax import lax
from jax.experimental import pallas as pl
from jax.experimental.pallas import tpu as pltpu

D_MODEL = 2048
HEAD_DIM = 128
N_HEADS_TOTAL = D_MODEL // HEAD_DIM
A_HEADS = N_HEADS_TOTAL // 2
A_KV_HEADS = 2
A_GROUP = A_HEADS // A_KV_HEADS
WINDOW = 128
BLOCK = 128
B_HEADS = N_HEADS_TOTAL - A_HEADS
B_QK_DIM = HEAD_DIM // 2
B_V_DIM = HEAD_DIM
MIX_WIDTH = A_HEADS * HEAD_DIM + B_HEADS * B_V_DIM
D_FF = 4 * D_MODEL
EPS = 1e-5
A_Q_W = A_HEADS * HEAD_DIM
A_KV_W = A_KV_HEADS * HEAD_DIM
B_QK_W = B_HEADS * 2 * B_QK_DIM
B_V_W = B_HEADS * B_V_DIM
IN_WIDTH = A_Q_W + 2 * A_KV_W + 2 * B_QK_W + B_V_W
LAMBDA_INIT = 0.8 - 0.6 * math.exp(-0.3 * 0)

_AK_BLK = A_Q_W // HEAD_DIM
_AV_BLK = (A_Q_W + A_KV_W) // HEAD_DIM
_BQ_BLK = (A_Q_W + 2 * A_KV_W) // HEAD_DIM
_BK_BLK = _BQ_BLK + B_QK_W // HEAD_DIM
_BV_BLK = _BK_BLK + B_QK_W // HEAD_DIM

_VMEM_LIMIT_BYTES = 56 * 1024 * 1024

_NT = (((1,), (1,)), ((), ()))

_BF16 = jnp.bfloat16
_F32 = jnp.float32
_LOG2E = math.log2(math.e)


def _rms_scale(x):
    return x * lax.rsqrt(jnp.mean(x * x, axis=-1, keepdims=True) + EPS)


def _inproj_body(x_ref, g_ref, w_ref, cs_ref, o_ref, h_ref):
    @pl.when(pl.program_id(1) == 0)
    def _():
        h_ref[...] = (_rms_scale(x_ref[...]) * g_ref[...]).astype(_BF16)

    acc = jnp.dot(h_ref[...], w_ref[...], preferred_element_type=_F32)
    o_ref[...] = (acc * cs_ref[...]).astype(o_ref.dtype)


def _inproj(x2d, g, w, colscale, *, tm=1024, tn=1536):
    t = x2d.shape[0]
    return pl.pallas_call(
        _inproj_body,
        grid=(t // tm, IN_WIDTH // tn),
        in_specs=[
            pl.BlockSpec((tm, D_MODEL), lambda i, j: (i, 0)),
            pl.BlockSpec((1, D_MODEL), lambda i, j: (0, 0)),
            pl.BlockSpec((D_MODEL, tn), lambda i, j: (0, j)),
            pl.BlockSpec((1, tn), lambda i, j: (0, j)),
        ],
        out_specs=pl.BlockSpec((tm, tn), lambda i, j: (i, j)),
        out_shape=jax.ShapeDtypeStruct((t, IN_WIDTH), _BF16),
        scratch_shapes=[pltpu.VMEM((tm, D_MODEL), _BF16)],
        compiler_params=pltpu.CompilerParams(
            dimension_semantics=("parallel", "arbitrary"),
            vmem_limit_bytes=_VMEM_LIMIT_BYTES),
        name="inproj",
    )(x2d, g, w, colscale)


_WIN_QBLOCKS = 8


def _winattn_body(par_ref, q_ref, kp_ref, kc_ref, kn_ref, vp_ref, vc_ref, vn_ref, o_ref, *, seq):
    i = pl.program_id(1)
    kv = pl.program_id(2)
    kall = jnp.concatenate([kp_ref[...], kc_ref[...], kn_ref[...]], axis=0)
    vall = jnp.concatenate([vp_ref[...], vc_ref[...], vn_ref[...]], axis=0)
    qi = lax.broadcasted_iota(jnp.int32, (BLOCK, 3 * BLOCK), 0)
    kj = lax.broadcasted_iota(jnp.int32, (BLOCK, 3 * BLOCK), 1)
    dist = jnp.abs(kj - BLOCK - qi)
    distf = dist.astype(_F32)
    in_window = dist <= WINDOW
    krow = lax.broadcasted_iota(jnp.int32, (1, 3 * BLOCK), 1)
    bias2, sink2 = [], []
    for g in range(A_GROUP):
        slope = par_ref[kv * A_GROUP + g]
        bias2.append(jnp.where(in_window, (-_LOG2E * slope) * distf, -jnp.inf))
        sink2.append(_LOG2E * par_ref[A_HEADS + kv * A_GROUP + g])

    def scores(blk):
        rows = slice(blk * BLOCK, (blk + 1) * BLOCK)
        q = jnp.concatenate([q_ref[rows, g * HEAD_DIM:(g + 1) * HEAD_DIM] for g in range(A_GROUP)], axis=0)
        return lax.dot_general(q, kall[blk * BLOCK:(blk + 3) * BLOCK], _NT, preferred_element_type=_F32)

    def softmax(blk, s):
        kpos = (i * _WIN_QBLOCKS + blk - 1) * BLOCK + krow
        edge2 = jnp.where((kpos >= 0) & (kpos < seq), 0.0, -jnp.inf)
        ps = []
        for g in range(A_GROUP):
            t = s[g * BLOCK:(g + 1) * BLOCK] * (_LOG2E / math.sqrt(HEAD_DIM)) + bias2[g] + edge2
            m = jnp.maximum(jnp.max(t, axis=-1, keepdims=True), sink2[g])
            p = jnp.exp2(t - m)
            den = jnp.sum(p, axis=-1, keepdims=True) + jnp.exp2(sink2[g] - m)
            ps.append((p * (1.0 / den)).astype(_BF16))
        return jnp.concatenate(ps, axis=0)

    def values(blk, p):
        rows = slice(blk * BLOCK, (blk + 1) * BLOCK)
        o = jnp.dot(p, vall[blk * BLOCK:(blk + 3) * BLOCK], preferred_element_type=_F32)
        for g in range(A_GROUP):
            o_ref[rows, g * HEAD_DIM:(g + 1) * HEAD_DIM] = o[g * BLOCK:(g + 1) * BLOCK].astype(o_ref.dtype)

    s_cur = scores(0)
    p_prev = None
    for blk in range(_WIN_QBLOCKS):
        s_next = scores(blk + 1) if blk + 1 < _WIN_QBLOCKS else None
        p_cur = softmax(blk, s_cur)
        if p_prev is not None:
            values(blk - 1, p_prev)
        s_cur, p_prev = s_next, p_cur
    values(_WIN_QBLOCKS - 1, p_prev)


def _winattn(proj, params_a, *, batch, seq):
    t = batch * seq
    nb = seq // BLOCK
    qb = _WIN_QBLOCKS
    assert nb % qb == 0
    nq = nb // qb
    qw = A_GROUP * HEAD_DIM

    def edge_spec(col0, first):
        def imap(b, i, kv):
            n = jnp.clip(i * qb - 1, 0, nb - 1) if first else jnp.clip((i + 1) * qb, 0, nb - 1)
            return (b * nb + n, col0 + kv)
        return pl.BlockSpec((BLOCK, HEAD_DIM), imap)

    def mid_spec(col0):
        return pl.BlockSpec((qb * BLOCK, HEAD_DIM), lambda b, i, kv: (b * nq + i, col0 + kv))

    return pl.pallas_call(
        functools.partial(_winattn_body, seq=seq),
        grid=(batch, nq, A_KV_HEADS),
        in_specs=[
            pl.BlockSpec(memory_space=pltpu.SMEM),
            pl.BlockSpec((qb * BLOCK, qw), lambda b, i, kv: (b * nq + i, kv)),
            edge_spec(_AK_BLK, True), mid_spec(_AK_BLK), edge_spec(_AK_BLK, False),
            edge_spec(_AV_BLK, True), mid_spec(_AV_BLK), edge_spec(_AV_BLK, False),
        ],
        out_specs=pl.BlockSpec((qb * BLOCK, qw), lambda b, i, kv: (b * nq + i, kv)),
        out_shape=jax.ShapeDtypeStruct((t, A_Q_W), _BF16),
        compiler_params=pltpu.CompilerParams(
            dimension_semantics=("parallel", "parallel", "parallel")),
        name="winattn",
    )(params_a, proj, proj, proj, proj, proj, proj, proj)


_ONES_ROWS = 16
_N_SPLIT = 3
_EXP_ZERO = 110.0
_NORM_SLACK = 2.1
_Q_SUBTILES = 4
_FIXED_MAX_SPAN = 60.0


def _split_bf16(t):
    pieces = []
    for _ in range(_N_SPLIT):
        piece = t.astype(_BF16)
        pieces.append(piece)
        t = t - piece.astype(_F32)
    return pieces


def _diffattn_body(par_ref, q_ref, k_ref, v_ref, lq1_ref, lk1_ref, lq2_ref, lk2_ref, g_ref, o_ref,
                   qm_ref, ka_ref, vt_ref, kn_ref, nb_ref, a0_ref, a1_ref, p0_ref, p1_ref, m_ref, acc_ref, *, tq, tk, seq):
    h = pl.program_id(1)
    qi = pl.program_id(2)
    nk = seq // tk
    slope = par_ref[h]
    inv_slope = par_ref[B_HEADS + h]
    lane_k = lax.broadcasted_iota(jnp.int32, (tk, HEAD_DIM), 1)

    def half_sq_norms_max(n2):
        return jnp.max(jnp.max(n2, axis=0, keepdims=True), axis=1, keepdims=True)

    def max_sq_norm(x):
        sq = x.astype(_F32) ** 2
        lane = lax.broadcasted_iota(jnp.int32, sq.shape, 1)
        s0 = jnp.sum(jnp.where(lane < B_QK_DIM, sq, 0.0), axis=1, keepdims=True)
        s1 = jnp.sum(jnp.where(lane >= B_QK_DIM, sq, 0.0), axis=1, keepdims=True)
        return half_sq_norms_max(jnp.maximum(s0, s1))

    def max_sq_norm_mxu(x):
        sq = (x.astype(_F32) ** 2).astype(_BF16)
        row = lax.broadcasted_iota(jnp.int32, (HEAD_DIM, HEAD_DIM), 0)
        colm = lax.broadcasted_iota(jnp.int32, (HEAD_DIM, HEAD_DIM), 1)
        half_sum = jnp.where((row // B_QK_DIM) == colm, 1.0, 0.0).astype(_BF16)
        return half_sq_norms_max(jnp.dot(sq, half_sum, preferred_element_type=_F32))

    @pl.when(qi == 0)
    def _():
        r = lax.broadcasted_iota(jnp.int32, (tk, HEAD_DIM), 0).astype(_F32)
        feats = []
        for mp in range(2):
            base = (1 - mp) * B_QK_DIM
            f = jnp.zeros((tk, HEAD_DIM), _F32)
            for c, piece in enumerate(_split_bf16(slope * r)):
                f = jnp.where(lane_k == base + c, piece.astype(_F32), f)
            feats.append(f)
        ones = jnp.ones((_ONES_ROWS, tk), _BF16)
        kn2 = jnp.zeros((1, 1), _F32)
        for t in range(nk):
            vt_ref[t] = jnp.concatenate([v_ref[t * tk:(t + 1) * tk, :].T, ones], axis=0)
            k = k_ref[t * tk:(t + 1) * tk, :]
            kn2 = jnp.maximum(kn2, max_sq_norm_mxu(k))
            ka_ref[0, t * tk:(t + 1) * tk, :] = jnp.where(lane_k < B_QK_DIM, k.astype(_F32), feats[0]).astype(_BF16)
            ka_ref[1, t * tk:(t + 1) * tk, :] = jnp.where(lane_k >= B_QK_DIM, k.astype(_F32), feats[1]).astype(_BF16)
        kn_ref[...] = kn2
        if tq == tk:
            krow = lax.broadcasted_iota(jnp.int32, (tk, tq), 0)
            qcol = lax.broadcasted_iota(jnp.int32, (tk, tq), 1)
            nb_ref[...] = slope * jnp.abs(qcol - krow).astype(_F32)

    def query_tile(sub, carry):
        rows = pl.ds(pl.multiple_of(sub * tq, tq), tq)
        q = q_ref[rows, :]
        lane = lax.broadcasted_iota(jnp.int32, (tq, HEAD_DIM), 1)
        first_half = lane < B_QK_DIM
        for mp in range(2):
            own = first_half if mp == 0 else jnp.logical_not(first_half)
            base = (1 - mp) * B_QK_DIM
            sel = jnp.where((lane >= base) & (lane < base + _N_SPLIT), 1.0, 0.0)
            for kind, cols in enumerate((sel, -sel, jnp.zeros_like(sel))):
                qm_ref[kind, mp] = jnp.where(own, q.astype(_F32), cols).astype(_BF16)
        acc_ref[...] = jnp.zeros_like(acc_ref)

        col = lax.broadcasted_iota(jnp.int32, (1, 2 * tq), 1)
        q0 = (qi * _Q_SUBTILES + sub) * tq
        qpos = (q0 + jnp.where(col < tq, col, col - tq)).astype(_F32)

        def update(a, c, vt):
            m_old = m_ref[...]
            m_new = jnp.maximum(m_old, jnp.max(a, axis=0, keepdims=True) + c)
            alpha = jnp.exp(m_old - m_new)
            p = jnp.exp(a - (m_new - c))
            pv = jnp.dot(vt, p.astype(_BF16), preferred_element_type=_F32)
            acc_ref[...] = alpha * acc_ref[...] + pv
            m_ref[...] = m_new

        kd = q0 // tk
        span = _NORM_SLACK * jnp.sqrt(max_sq_norm(q) * kn_ref[...])
        reach = (_EXP_ZERO + span) * inv_slope
        reach = jnp.where(reach < seq, reach, float(seq)).astype(jnp.int32)[0, 0] + 1
        lo = jnp.clip((q0 - reach) // tk, 0, kd)
        hi = jnp.clip((q0 + tq - 1 + reach) // tk, kd, nk - 1)
        even = (hi - lo) % 2 == 0
        grow_hi = even & (hi < nk - 1)
        hi = hi + grow_hi.astype(jnp.int32)
        lo = lo - (even & jnp.logical_not(grow_hi)).astype(jnp.int32)
        n_far = hi - lo

        def far_index(t):
            ki = lo + t
            ki = ki + (ki >= kd).astype(jnp.int32)
            return ki, (ki > kd).astype(jnp.int32), pl.multiple_of(ki * tk, tk)

        def scores(k0, kind, a_ref):
            for mp in range(2):
                a_ref[:, mp * tq:(mp + 1) * tq] = lax.dot_general(
                    ka_ref[mp, pl.ds(k0, tk), :], qm_ref[kind, mp], _NT, preferred_element_type=_F32)

        def far_scores(t, a_ref):
            _, after, k0 = far_index(t)
            scores(k0, after, a_ref)

        def far_fold(t, a_ref):
            ki, after, k0 = far_index(t)
            sign = (1 - 2 * after).astype(_F32)
            c = (sign * slope) * (k0.astype(_F32) - qpos)
            update(a_ref[...], c, vt_ref[ki])

        def near_bias():
            if tq == tk:
                return nb_ref[...]
            krow = lax.broadcasted_iota(jnp.int32, (tk, tq), 0)
            qcol = lax.broadcasted_iota(jnp.int32, (tk, tq), 1)
            return slope * jnp.abs(qcol - krow + (q0 - kd * tk)).astype(_F32)

        def pair_loops(pair):
            def grouped(count):
                def body(j, carry):
                    for u in range(count):
                        pair(count * j + u)
                    return carry
                return body

            n_pairs = (n_far - 1) // 2
            done = jnp.int32(0)
            for count in (4, 2, 1):
                trips = (n_pairs - done) // count
                lax.fori_loop(done // count, done // count + trips, grouped(count), 0)
                done = done + trips * count

        fixed_reference = jnp.where(span <= _FIXED_MAX_SPAN, 1, 0)[0, 0] == 1

        @pl.when(fixed_reference)
        def _():
            ref = 0.5 * span

            def probs(k0, kind, shift, p_ref):
                for mp in range(2):
                    cols = slice(mp * tq, (mp + 1) * tq)
                    a = lax.dot_general(ka_ref[mp, pl.ds(k0, tk), :], qm_ref[kind, mp], _NT,
                                        preferred_element_type=_F32)
                    sh = shift if shift.shape[0] == tk else shift[:, cols]
                    p_ref[:, cols] = jnp.exp(a - sh).astype(_BF16)

            def far_probs(t, p_ref):
                _, after, k0 = far_index(t)
                sign = (1 - 2 * after).astype(_F32)
                probs(k0, after, ref - (sign * slope) * (k0.astype(_F32) - qpos), p_ref)

            def values(ki, p_ref):
                acc_ref[...] += jnp.dot(vt_ref[ki], p_ref[...], preferred_element_type=_F32)

            def far_tile(t):
                return far_index(t)[0]

            probs(pl.multiple_of(kd * tk, tk), 2, near_bias() + ref, p0_ref)

            def pair(j):
                t = 2 * j
                far_probs(t, p1_ref)
                values(jnp.where(j == 0, kd, far_tile(t - 1)), p0_ref)
                far_probs(t + 1, p0_ref)
                values(far_tile(t), p1_ref)

            pair_loops(pair)
            far_probs(n_far - 1, p1_ref)
            values(jnp.where(n_far == 1, kd, far_tile(n_far - 2)), p0_ref)
            values(far_tile(n_far - 1), p1_ref)

        @pl.when(jnp.logical_not(fixed_reference))
        def _():
            m_ref[...] = jnp.full_like(m_ref, -jnp.inf)
            scores(pl.multiple_of(kd * tk, tk), 2, a0_ref)
            far_scores(jnp.int32(0), a1_ref)
            bias = near_bias()
            update(a0_ref[...] - jnp.concatenate([bias, bias], axis=1), jnp.zeros((1, 2 * tq), _F32), vt_ref[kd])

            def pair(j):
                t = 2 * j
                far_scores(t + 1, a0_ref)
                far_fold(t, a1_ref)
                far_scores(t + 2, a1_ref)
                far_fold(t + 1, a0_ref)

            pair_loops(pair)
            far_fold(n_far - 1, a1_ref)

        lam = (jnp.exp(jnp.sum(lq1_ref[...] * lk1_ref[...], axis=-1, keepdims=True))
               - jnp.exp(jnp.sum(lq2_ref[...] * lk2_ref[...], axis=-1, keepdims=True))
               + LAMBDA_INIT)
        acc = acc_ref[...]
        on = acc[0:B_V_DIM] / acc[B_V_DIM:B_V_DIM + 1]
        o = on[:, 0:tq] - lam * on[:, tq:2 * tq]
        y = o * lax.rsqrt(jnp.mean(o * o, axis=0, keepdims=True) + EPS)
        o_ref[rows, :] = ((y.T * g_ref[...]) * (1.0 - LAMBDA_INIT)).astype(o_ref.dtype)
        return carry

    lax.fori_loop(0, _Q_SUBTILES, query_tile, 0)


def _diffattn(proj, params_b, lq1, lk1, lq2, lk2, subln_g, *, batch, seq, tq=512, tk=512):
    t = batch * seq
    tqs = tq * _Q_SUBTILES
    nq = seq // tqs
    nk = seq // tk
    assert tk % tq == 0 and seq % tk == 0 and nk % 2 == 0 and nk >= 2 and seq % tqs == 0
    small = lambda shape: pl.BlockSpec(shape, lambda b, h, i: (0, 0))
    return pl.pallas_call(
        functools.partial(_diffattn_body, tq=tq, tk=tk, seq=seq),
        grid=(batch, B_HEADS, nq),
        in_specs=[
            pl.BlockSpec(memory_space=pltpu.SMEM),
            pl.BlockSpec((tqs, HEAD_DIM), lambda b, h, i: (b * nq + i, _BQ_BLK + h)),
            pl.BlockSpec((seq, HEAD_DIM), lambda b, h, i: (b, _BK_BLK + h)),
            pl.BlockSpec((seq, HEAD_DIM), lambda b, h, i: (b, _BV_BLK + h)),
            small((1, B_QK_DIM)), small((1, B_QK_DIM)), small((1, B_QK_DIM)), small((1, B_QK_DIM)),
            small((1, B_V_DIM)),
        ],
        out_specs=pl.BlockSpec((tqs, B_V_DIM), lambda b, h, i: (b * nq + i, h)),
        out_shape=jax.ShapeDtypeStruct((t, B_V_W), _BF16),
        scratch_shapes=[
            pltpu.VMEM((3, 2, tq, HEAD_DIM), _BF16),
            pltpu.VMEM((2, seq, HEAD_DIM), _BF16),
            pltpu.VMEM((nk, B_V_DIM + _ONES_ROWS, tk), _BF16),
            pltpu.VMEM((1, 1), _F32),
            pltpu.VMEM((tk, tq) if tq == tk else (8, 128), _F32),
            pltpu.VMEM((tk, 2 * tq), _F32),
            pltpu.VMEM((tk, 2 * tq), _F32),
            pltpu.VMEM((tk, 2 * tq), _BF16),
            pltpu.VMEM((tk, 2 * tq), _BF16),
            pltpu.VMEM((1, 2 * tq), _F32),
            pltpu.VMEM((B_V_DIM + _ONES_ROWS, 2 * tq), _F32),
        ],
        compiler_params=pltpu.CompilerParams(
            dimension_semantics=("parallel", "parallel", "arbitrary"),
            vmem_limit_bytes=_VMEM_LIMIT_BYTES),
        name="diffattn",
    )(params_b, proj, proj, proj, lq1, lk1, lq2, lk2, subln_g)


def _outproj_body(oa_ref, ob_ref, w_ref, x_ref, o_ref):
    a = jnp.concatenate([oa_ref[...], ob_ref[...]], axis=1)
    o_ref[...] = x_ref[...] + jnp.dot(a, w_ref[...], preferred_element_type=_F32)


def _outproj(oa, ob, w, x2d, *, tm=512):
    t = x2d.shape[0]
    return pl.pallas_call(
        _outproj_body,
        grid=(t // tm,),
        in_specs=[
            pl.BlockSpec((tm, A_Q_W), lambda i: (i, 0)),
            pl.BlockSpec((tm, B_V_W), lambda i: (i, 0)),
            pl.BlockSpec((MIX_WIDTH, D_MODEL), lambda i: (0, 0)),
            pl.BlockSpec((tm, D_MODEL), lambda i: (i, 0)),
        ],
        out_specs=pl.BlockSpec((tm, D_MODEL), lambda i: (i, 0)),
        out_shape=jax.ShapeDtypeStruct((t, D_MODEL), _F32),
        compiler_params=pltpu.CompilerParams(
            dimension_semantics=("parallel",),
            vmem_limit_bytes=_VMEM_LIMIT_BYTES),
        name="outproj",
    )(oa, ob, w, x2d)


def _mlp_body(x_ref, gm_ref, wu_ref, wd_ref, gf_ref, o_ref, h_ref, acc_ref):
    j = pl.program_id(1)

    @pl.when(j == 0)
    def _():
        x = x_ref[...]
        h_ref[...] = (_rms_scale(x) * gm_ref[...]).astype(_BF16)
        acc_ref[...] = x

    u = jnp.maximum(jnp.dot(h_ref[...], wu_ref[...], preferred_element_type=_F32), 0.0)
    acc_ref[...] += jnp.dot((u * u).astype(_BF16), wd_ref[...], preferred_element_type=_F32)

    @pl.when(j == pl.num_programs(1) - 1)
    def _():
        o_ref[...] = _rms_scale(acc_ref[...]) * gf_ref[...]


def _mlp(x1, g_mlp, w_up, w_down, g_final, *, tm=512, tf=1024):
    t = x1.shape[0]
    return pl.pallas_call(
        _mlp_body,
        grid=(t // tm, D_FF // tf),
        in_specs=[
            pl.BlockSpec((tm, D_MODEL), lambda i, j: (i, 0)),
            pl.BlockSpec((1, D_MODEL), lambda i, j: (0, 0)),
            pl.BlockSpec((D_MODEL, tf), lambda i, j: (0, j)),
            pl.BlockSpec((tf, D_MODEL), lambda i, j: (j, 0)),
            pl.BlockSpec((1, D_MODEL), lambda i, j: (0, 0)),
        ],
        out_specs=pl.BlockSpec((tm, D_MODEL), lambda i, j: (i, 0)),
        out_shape=jax.ShapeDtypeStruct((t, D_MODEL), _F32),
        scratch_shapes=[pltpu.VMEM((tm, D_MODEL), _BF16), pltpu.VMEM((tm, D_MODEL), _F32)],
        compiler_params=pltpu.CompilerParams(
            dimension_semantics=("parallel", "arbitrary"),
            vmem_limit_bytes=_VMEM_LIMIT_BYTES),
        name="mlp",
    )(x1, g_mlp, w_up, w_down, g_final)


def _alibi_slopes():
    i = jnp.arange(1, N_HEADS_TOTAL + 1, dtype=_F32)
    s = jnp.exp2(-8.0 / N_HEADS_TOTAL * i)
    return s[0::2], s[1::2]


def _encoder(x, p):
    batch, seq = x.shape[0], x.shape[1]
    x2d = x.reshape(batch * seq, D_MODEL)
    proj = _inproj(x2d, p["g_attn"], p["w_in"], p["colscale"])
    oa = _winattn(proj, p["params_a"], batch=batch, seq=seq)
    ob = _diffattn(proj, p["params_b"], p["lq1"], p["lk1"], p["lq2"], p["lk2"], p["subln_g"],
                   batch=batch, seq=seq)
    x1 = _outproj(oa, ob, p["w_out"], x2d)
    y = _mlp(x1, p["g_mlp"], p["w_up"], p["w_down"], p["g_final"])
    return y.reshape(batch, seq, D_MODEL)


def kernel(x_prompt, x_sample, norm_attn_g, w_in, sink_logits, lambda_q1, lambda_k1, lambda_q2, lambda_k2,
           diff_subln_g, w_out, norm_mlp_g, w_up, w_down, norm_final_g):
    slopes_a, slopes_b = _alibi_slopes()
    col = jnp.arange(IN_WIDTH)
    in_bq = (col >= _BQ_BLK * HEAD_DIM) & (col < _BK_BLK * HEAD_DIM)
    p = {
        "g_attn": norm_attn_g[0].reshape(1, D_MODEL).astype(_F32),
        "w_in": w_in[0].astype(_BF16),
        "colscale": jnp.where(in_bq, 1.0 / math.sqrt(B_QK_DIM), 1.0).astype(_F32).reshape(1, IN_WIDTH),
        "params_a": jnp.concatenate([slopes_a, sink_logits[0].astype(_F32)]),
        "params_b": jnp.concatenate([slopes_b, 1.0 / slopes_b]),
        "lq1": lambda_q1[0].reshape(1, B_QK_DIM).astype(_F32),
        "lk1": lambda_k1[0].reshape(1, B_QK_DIM).astype(_F32),
        "lq2": lambda_q2[0].reshape(1, B_QK_DIM).astype(_F32),
        "lk2": lambda_k2[0].reshape(1, B_QK_DIM).astype(_F32),
        "subln_g": diff_subln_g[0].reshape(1, B_V_DIM).astype(_F32),
        "w_out": w_out[0].astype(_BF16),
        "g_mlp": norm_mlp_g[0].reshape(1, D_MODEL).astype(_F32),
        "w_up": w_up[0].astype(_BF16),
        "w_down": w_down[0].astype(_BF16),
        "g_final": norm_final_g.reshape(1, D_MODEL).astype(_F32),
    }
    return (_encoder(x_prompt, p), _encoder(x_sample, p))
```

```python
import functools
import math

import jax
import jax.numpy as jnp
from jax import lax
from jax.experimental import pallas as pl
from jax.experimental.pallas import tpu as pltpu

D_MODEL = 2048
HEAD_DIM = 128
N_HEADS_TOTAL = D_MODEL // HEAD_DIM
A_HEADS = N_HEADS_TOTAL // 2
A_KV_HEADS = 2
A_GROUP = A_HEADS // A_KV_HEADS
WINDOW = 128
BLOCK = 128
B_HEADS = N_HEADS_TOTAL - A_HEADS
B_QK_DIM = HEAD_DIM // 2
B_V_DIM = HEAD_DIM
MIX_WIDTH = A_HEADS * HEAD_DIM + B_HEADS * B_V_DIM
D_FF = 4 * D_MODEL
EPS = 1e-5
A_Q_W = A_HEADS * HEAD_DIM
A_KV_W = A_KV_HEADS * HEAD_DIM
B_QK_W = B_HEADS * 2 * B_QK_DIM
B_V_W = B_HEADS * B_V_DIM
IN_WIDTH = A_Q_W + 2 * A_KV_W + 2 * B_QK_W + B_V_W
LAMBDA_INIT = 0.8 - 0.6 * math.exp(-0.3 * 0)

_AK_BLK = A_Q_W // HEAD_DIM
_AV_BLK = (A_Q_W + A_KV_W) // HEAD_DIM
_BQ_BLK = (A_Q_W + 2 * A_KV_W) // HEAD_DIM
_BK_BLK = _BQ_BLK + B_QK_W // HEAD_DIM
_BV_BLK = _BK_BLK + B_QK_W // HEAD_DIM

_VMEM_LIMIT_BYTES = 56 * 1024 * 1024

_NT = (((1,), (1,)), ((), ()))

_BF16 = jnp.bfloat16
_F32 = jnp.float32
_LOG2E = math.log2(math.e)


def _rms_scale(x):
    return x * lax.rsqrt(jnp.mean(x * x, axis=-1, keepdims=True) + EPS)


def _inproj_body(x_ref, g_ref, w_ref, cs_ref, o_ref, h_ref):
    @pl.when(pl.program_id(1) == 0)
    def _():
        h_ref[...] = (_rms_scale(x_ref[...]) * g_ref[...]).astype(_BF16)

    acc = jnp.dot(h_ref[...], w_ref[...], preferred_element_type=_F32)
    o_ref[...] = (acc * cs_ref[...]).astype(o_ref.dtype)


def _inproj(x2d, g, w, colscale, *, tm=1024, tn=1536):
    t = x2d.shape[0]
    return pl.pallas_call(
        _inproj_body,
        grid=(t // tm, IN_WIDTH // tn),
        in_specs=[
            pl.BlockSpec((tm, D_MODEL), lambda i, j: (i, 0)),
            pl.BlockSpec((1, D_MODEL), lambda i, j: (0, 0)),
            pl.BlockSpec((D_MODEL, tn), lambda i, j: (0, j)),
            pl.BlockSpec((1, tn), lambda i, j: (0, j)),
        ],
        out_specs=pl.BlockSpec((tm, tn), lambda i, j: (i, j)),
        out_shape=jax.ShapeDtypeStruct((t, IN_WIDTH), _BF16),
        scratch_shapes=[pltpu.VMEM((tm, D_MODEL), _BF16)],
        compiler_params=pltpu.CompilerParams(
            dimension_semantics=("parallel", "arbitrary"),
            vmem_limit_bytes=_VMEM_LIMIT_BYTES),
        name="inproj",
    )(x2d, g, w, colscale)


_WIN_QBLOCKS = 8


def _winattn_body(par_ref, q_ref, kp_ref, kc_ref, kn_ref, vp_ref, vc_ref, vn_ref, o_ref, *, seq):
    i = pl.program_id(1)
    kv = pl.program_id(2)
    kall = jnp.concatenate([kp_ref[...], kc_ref[...], kn_ref[...]], axis=0)
    vall = jnp.concatenate([vp_ref[...], vc_ref[...], vn_ref[...]], axis=0)
    qi = lax.broadcasted_iota(jnp.int32, (BLOCK, 3 * BLOCK), 0)
    kj = lax.broadcasted_iota(jnp.int32, (BLOCK, 3 * BLOCK), 1)
    dist = jnp.abs(kj - BLOCK - qi)
    distf = dist.astype(_F32)
    in_window = dist <= WINDOW
    krow = lax.broadcasted_iota(jnp.int32, (1, 3 * BLOCK), 1)
    bias2, sink2 = [], []
    for g in range(A_GROUP):
        slope = par_ref[kv * A_GROUP + g]
        bias2.append(jnp.where(in_window, (-_LOG2E * slope) * distf, -jnp.inf))
        sink2.append(_LOG2E * par_ref[A_HEADS + kv * A_GROUP + g])

    def scores(blk):
        rows = slice(blk * BLOCK, (blk + 1) * BLOCK)
        q = jnp.concatenate([q_ref[rows, g * HEAD_DIM:(g + 1) * HEAD_DIM] for g in range(A_GROUP)], axis=0)
        return lax.dot_general(q, kall[blk * BLOCK:(blk + 3) * BLOCK], _NT, preferred_element_type=_F32)

    def softmax(blk, s):
        kpos = (i * _WIN_QBLOCKS + blk - 1) * BLOCK + krow
        edge2 = jnp.where((kpos >= 0) & (kpos < seq), 0.0, -jnp.inf)
        ps = []
        for g in range(A_GROUP):
            t = s[g * BLOCK:(g + 1) * BLOCK] * (_LOG2E / math.sqrt(HEAD_DIM)) + bias2[g] + edge2
            m = jnp.maximum(jnp.max(t, axis=-1, keepdims=True), sink2[g])
            p = jnp.exp2(t - m)
            den = jnp.sum(p, axis=-1, keepdims=True) + jnp.exp2(sink2[g] - m)
            ps.append((p * (1.0 / den)).astype(_BF16))
        return jnp.concatenate(ps, axis=0)

    def values(blk, p):
        rows = slice(blk * BLOCK, (blk + 1) * BLOCK)
        o = jnp.dot(p, vall[blk * BLOCK:(blk + 3) * BLOCK], preferred_element_type=_F32)
        for g in range(A_GROUP):
            o_ref[rows, g * HEAD_DIM:(g + 1) * HEAD_DIM] = o[g * BLOCK:(g + 1) * BLOCK].astype(o_ref.dtype)

    s_cur = scores(0)
    p_prev = None
    for blk in range(_WIN_QBLOCKS):
        s_next = scores(blk + 1) if blk + 1 < _WIN_QBLOCKS else None
        p_cur = softmax(blk, s_cur)
        if p_prev is not None:
            values(blk - 1, p_prev)
        s_cur, p_prev = s_next, p_cur
    values(_WIN_QBLOCKS - 1, p_prev)


def _winattn(proj, params_a, *, batch, seq):
    t = batch * seq
    nb = seq // BLOCK
    qb = _WIN_QBLOCKS
    assert nb % qb == 0
    nq = nb // qb
    qw = A_GROUP * HEAD_DIM

    def edge_spec(col0, first):
        def imap(b, i, kv):
            n = jnp.clip(i * qb - 1, 0, nb - 1) if first else jnp.clip((i + 1) * qb, 0, nb - 1)
            return (b * nb + n, col0 + kv)
        return pl.BlockSpec((BLOCK, HEAD_DIM), imap)

    def mid_spec(col0):
        return pl.BlockSpec((qb * BLOCK, HEAD_DIM), lambda b, i, kv: (b * nq + i, col0 + kv))

    return pl.pallas_call(
        functools.partial(_winattn_body, seq=seq),
        grid=(batch, nq, A_KV_HEADS),
        in_specs=[
            pl.BlockSpec(memory_space=pltpu.SMEM),
            pl.BlockSpec((qb * BLOCK, qw), lambda b, i, kv: (b * nq + i, kv)),
            edge_spec(_AK_BLK, True), mid_spec(_AK_BLK), edge_spec(_AK_BLK, False),
            edge_spec(_AV_BLK, True), mid_spec(_AV_BLK), edge_spec(_AV_BLK, False),
        ],
        out_specs=pl.BlockSpec((qb * BLOCK, qw), lambda b, i, kv: (b * nq + i, kv)),
        out_shape=jax.ShapeDtypeStruct((t, A_Q_W), _BF16),
        compiler_params=pltpu.CompilerParams(
            dimension_semantics=("parallel", "parallel", "parallel")),
        name="winattn",
    )(params_a, proj, proj, proj, proj, proj, proj, proj)


_ONES_ROWS = 16
_N_SPLIT = 3
_EXP_ZERO = 110.0
_NORM_SLACK = 2.1
_Q_SUBTILES = 4
_FIXED_MAX_SPAN = 60.0


def _split_bf16(t):
    pieces = []
    for _ in range(_N_SPLIT):
        piece = t.astype(_BF16)
        pieces.append(piece)
        t = t - piece.astype(_F32)
    return pieces


def _diffattn_body(par_ref, q_ref, k_ref, v_ref, vec_ref, o_ref,
                   qm_ref, ka_ref, vt_ref, nb_ref, a0_ref, a1_ref, p0_ref, p1_ref, acc_ref, m_ref, kn_ref, *, tq, tk, seq):
    h = pl.program_id(1)
    qi = pl.program_id(2)
    nk = seq // tk
    slope = par_ref[h]
    inv_slope = par_ref[B_HEADS + h]
    lane_k = lax.broadcasted_iota(jnp.int32, (tk, HEAD_DIM), 1)

    def half_sq_norms_max(n2):
        return jnp.max(jnp.max(n2, axis=0, keepdims=True), axis=1, keepdims=True)

    def max_sq_norm(x):
        sq = x.astype(_F32) ** 2
        lane = lax.broadcasted_iota(jnp.int32, sq.shape, 1)
        s0 = jnp.sum(jnp.where(lane < B_QK_DIM, sq, 0.0), axis=1, keepdims=True)
        s1 = jnp.sum(jnp.where(lane >= B_QK_DIM, sq, 0.0), axis=1, keepdims=True)
        return half_sq_norms_max(jnp.maximum(s0, s1))

    def max_sq_norm_mxu(x):
        sq = (x.astype(_F32) ** 2).astype(_BF16)
        row = lax.broadcasted_iota(jnp.int32, (HEAD_DIM, HEAD_DIM), 0)
        colm = lax.broadcasted_iota(jnp.int32, (HEAD_DIM, HEAD_DIM), 1)
        half_sum = jnp.where((row // B_QK_DIM) == colm, 1.0, 0.0).astype(_BF16)
        return half_sq_norms_max(jnp.dot(sq, half_sum, preferred_element_type=_F32))

    @pl.when(qi == 0)
    def _():
        r = lax.broadcasted_iota(jnp.int32, (tk, HEAD_DIM), 0).astype(_F32)
        feats = []
        for mp in range(2):
            base = (1 - mp) * B_QK_DIM
            f = jnp.zeros((tk, HEAD_DIM), _F32)
            for c, piece in enumerate(_split_bf16(slope * r)):
                f = jnp.where(lane_k == base + c, piece.astype(_F32), f)
            feats.append(f)
        ones = jnp.ones((_ONES_ROWS, tk), _BF16)
        kn2 = jnp.zeros((1, 1), _F32)
        for t in range(nk):
            vt_ref[t] = jnp.concatenate([v_ref[t * tk:(t + 1) * tk, :].T, ones], axis=0)
            k = k_ref[t * tk:(t + 1) * tk, :]
            kn2 = jnp.maximum(kn2, max_sq_norm_mxu(k))
            ka_ref[0, t * tk:(t + 1) * tk, :] = jnp.where(lane_k < B_QK_DIM, k.astype(_F32), feats[0]).astype(_BF16)
            ka_ref[1, t * tk:(t + 1) * tk, :] = jnp.where(lane_k >= B_QK_DIM, k.astype(_F32), feats[1]).astype(_BF16)
        kn_ref[...] = kn2
        if tq == tk:
            krow = lax.broadcasted_iota(jnp.int32, (tk, tq), 0)
            qcol = lax.broadcasted_iota(jnp.int32, (tk, tq), 1)
            nb_ref[...] = slope * jnp.abs(qcol - krow).astype(_F32)

    def query_tile(sub, carry):
        rows = pl.ds(pl.multiple_of(sub * tq, tq), tq)
        q = q_ref[rows, :]
        lane = lax.broadcasted_iota(jnp.int32, (tq, HEAD_DIM), 1)
        first_half = lane < B_QK_DIM
        for mp in range(2):
            own = first_half if mp == 0 else jnp.logical_not(first_half)
            base = (1 - mp) * B_QK_DIM
            sel = jnp.where((lane >= base) & (lane < base + _N_SPLIT), 1.0, 0.0)
            for kind, cols in enumerate((sel, -sel, jnp.zeros_like(sel))):
                qm_ref[kind, mp] = jnp.where(own, q.astype(_F32), cols).astype(_BF16)
        acc_ref[...] = jnp.zeros_like(acc_ref)

        col = lax.broadcasted_iota(jnp.int32, (1, 2 * tq), 1)
        q0 = (qi * _Q_SUBTILES + sub) * tq
        qpos = (q0 + jnp.where(col < tq, col, col - tq)).astype(_F32)

        def update(a, c, vt):
            m_old = m_ref[...]
            m_new = jnp.maximum(m_old, jnp.max(a, axis=0, keepdims=True) + c)
            alpha = jnp.exp(m_old - m_new)
            p = jnp.exp(a - (m_new - c))
            pv = jnp.dot(vt, p.astype(_BF16), preferred_element_type=_F32)
            acc_ref[...] = alpha * acc_ref[...] + pv
            m_ref[...] = m_new

        kd = q0 // tk
        span = _NORM_SLACK * jnp.sqrt(max_sq_norm(q) * kn_ref[...])
        reach = (_EXP_ZERO + span) * inv_slope
        reach = jnp.where(reach < seq, reach, float(seq)).astype(jnp.int32)[0, 0] + 1
        lo = jnp.clip((q0 - reach) // tk, 0, kd)
        hi = jnp.clip((q0 + tq - 1 + reach) // tk, kd, nk - 1)
        even = (hi - lo) % 2 == 0
        grow_hi = even & (hi < nk - 1)
        hi = hi + grow_hi.astype(jnp.int32)
        lo = lo - (even & jnp.logical_not(grow_hi)).astype(jnp.int32)
        n_far = hi - lo

        def far_index(t):
            ki = lo + t
            ki = ki + (ki >= kd).astype(jnp.int32)
            return ki, (ki > kd).astype(jnp.int32), pl.multiple_of(ki * tk, tk)

        def scores(k0, kind, a_ref):
            for mp in range(2):
                a_ref[:, mp * tq:(mp + 1) * tq] = lax.dot_general(
                    ka_ref[mp, pl.ds(k0, tk), :], qm_ref[kind, mp], _NT, preferred_element_type=_F32)

        def far_scores(t, a_ref):
            _, after, k0 = far_index(t)
            scores(k0, after, a_ref)

        def far_fold(t, a_ref):
            ki, after, k0 = far_index(t)
            sign = (1 - 2 * after).astype(_F32)
            c = (sign * slope) * (k0.astype(_F32) - qpos)
            update(a_ref[...], c, vt_ref[ki])

        def near_bias():
            if tq == tk:
                return nb_ref[...]
            krow = lax.broadcasted_iota(jnp.int32, (tk, tq), 0)
            qcol = lax.broadcasted_iota(jnp.int32, (tk, tq), 1)
            return slope * jnp.abs(qcol - krow + (q0 - kd * tk)).astype(_F32)

        def pair_loops(pair):
            def grouped(count):
                def body(j, carry):
                    for u in range(count):
                        pair(count * j + u)
                    return carry
                return body

            n_pairs = (n_far - 1) // 2
            done = jnp.int32(0)
            for count in (4, 2, 1):
                trips = (n_pairs - done) // count
                lax.fori_loop(done // count, done // count + trips, grouped(count), 0)
                done = done + trips * count

        fixed_reference = jnp.where(span <= _FIXED_MAX_SPAN, 1, 0)[0, 0] == 1

        @pl.when(fixed_reference)
        def _():
            ref = 0.5 * span

            def probs(k0, kind, shift, p_ref):
                for mp in range(2):
                    cols = slice(mp * tq, (mp + 1) * tq)
                    a = lax.dot_general(ka_ref[mp, pl.ds(k0, tk), :], qm_ref[kind, mp], _NT,
                                        preferred_element_type=_F32)
                    sh = shift if shift.shape[0] == tk else shift[:, cols]
                    p_ref[:, cols] = jnp.exp(a - sh).astype(_BF16)

            def far_probs(t, p_ref):
                _, after, k0 = far_index(t)
                sign = (1 - 2 * after).astype(_F32)
                probs(k0, after, ref - (sign * slope) * (k0.astype(_F32) - qpos), p_ref)

            def values(ki, p_ref):
                acc_ref[...] += jnp.dot(vt_ref[ki], p_ref[...], preferred_element_type=_F32)

            def far_tile(t):
                return far_index(t)[0]

            probs(pl.multiple_of(kd * tk, tk), 2, near_bias() + ref, p0_ref)

            def pair(j):
                t = 2 * j
                far_probs(t, p1_ref)
                values(jnp.where(j == 0, kd, far_tile(t - 1)), p0_ref)
                far_probs(t + 1, p0_ref)
                values(far_tile(t), p1_ref)

            pair_loops(pair)
            far_probs(n_far - 1, p1_ref)
            values(jnp.where(n_far == 1, kd, far_tile(n_far - 2)), p0_ref)
            values(far_tile(n_far - 1), p1_ref)

        @pl.when(jnp.logical_not(fixed_reference))
        def _():
            m_ref[...] = jnp.full_like(m_ref, -jnp.inf)
            scores(pl.multiple_of(kd * tk, tk), 2, a0_ref)
            far_scores(jnp.int32(0), a1_ref)
            bias = near_bias()
            update(a0_ref[...] - jnp.concatenate([bias, bias], axis=1), jnp.zeros((1, 2 * tq), _F32), vt_ref[kd])

            def pair(j):
                t = 2 * j
                far_scores(t + 1, a0_ref)
                far_fold(t, a1_ref)
                far_scores(t + 2, a1_ref)
                far_fold(t + 1, a0_ref)

            pair_loops(pair)
            far_fold(n_far - 1, a1_ref)

        lq1, lk1, lq2, lk2 = (vec_ref[r:r + 1, 0:B_QK_DIM] for r in range(4))
        lam = (jnp.exp(jnp.sum(lq1 * lk1, axis=-1, keepdims=True))
               - jnp.exp(jnp.sum(lq2 * lk2, axis=-1, keepdims=True))
               + LAMBDA_INIT)
        acc = acc_ref[...]
        on = acc[0:B_V_DIM] / acc[B_V_DIM:B_V_DIM + 1]
        o = on[:, 0:tq] - lam * on[:, tq:2 * tq]
        y = o * lax.rsqrt(jnp.mean(o * o, axis=0, keepdims=True) + EPS)
        o_ref[rows, :] = ((y.T * vec_ref[4:5, :]) * (1.0 - LAMBDA_INIT)).astype(o_ref.dtype)
        return carry

    lax.fori_loop(0, _Q_SUBTILES, query_tile, 0)


def _diffattn(proj, params_b, vectors_b, *, batch, seq, tq=512, tk=512):
    t = batch * seq
    tqs = tq * _Q_SUBTILES
    nq = seq // tqs
    nk = seq // tk
    assert tk % tq == 0 and seq % tk == 0 and nk % 2 == 0 and nk >= 2 and seq % tqs == 0
    return pl.pallas_call(
        functools.partial(_diffattn_body, tq=tq, tk=tk, seq=seq),
        grid=(batch, B_HEADS, nq),
        in_specs=[
            pl.BlockSpec(memory_space=pltpu.SMEM),
            pl.BlockSpec((tqs, HEAD_DIM), lambda b, h, i: (b * nq + i, _BQ_BLK + h)),
            pl.BlockSpec((seq, HEAD_DIM), lambda b, h, i: (b, _BK_BLK + h)),
            pl.BlockSpec((seq, HEAD_DIM), lambda b, h, i: (b, _BV_BLK + h)),
            pl.BlockSpec(vectors_b.shape, lambda b, h, i: (0, 0)),
        ],
        out_specs=pl.BlockSpec((tqs, B_V_DIM), lambda b, h, i: (b * nq + i, h)),
        out_shape=jax.ShapeDtypeStruct((t, B_V_W), _BF16),
        scratch_shapes=[
            pltpu.VMEM((3, 2, tq, HEAD_DIM), _BF16),
            pltpu.VMEM((2, seq, HEAD_DIM), _BF16),
            pltpu.VMEM((nk, B_V_DIM + _ONES_ROWS, tk), _BF16),
            pltpu.VMEM((tk, tq) if tq == tk else (8, 128), _F32),
            pltpu.VMEM((tk, 2 * tq), _F32),
            pltpu.VMEM((tk, 2 * tq), _F32),
            pltpu.VMEM((tk, 2 * tq), _BF16),
            pltpu.VMEM((tk, 2 * tq), _BF16),
            pltpu.VMEM((B_V_DIM + _ONES_ROWS, 2 * tq), _F32),
            pltpu.VMEM((1, 2 * tq), _F32),
            pltpu.VMEM((1, 1), _F32),
        ],
        compiler_params=pltpu.CompilerParams(
            dimension_semantics=("parallel", "parallel", "arbitrary"),
            vmem_limit_bytes=_VMEM_LIMIT_BYTES),
        name="diffattn",
    )(params_b, proj, proj, proj, vectors_b)


def _outproj_body(oa_ref, ob_ref, w_ref, x_ref, o_ref):
    a = jnp.concatenate([oa_ref[...], ob_ref[...]], axis=1)
    o_ref[...] = x_ref[...] + jnp.dot(a, w_ref[...], preferred_element_type=_F32)


def _outproj(oa, ob, w, x2d, *, tm=512):
    t = x2d.shape[0]
    return pl.pallas_call(
        _outproj_body,
        grid=(t // tm,),
        in_specs=[
            pl.BlockSpec((tm, A_Q_W), lambda i: (i, 0)),
            pl.BlockSpec((tm, B_V_W), lambda i: (i, 0)),
            pl.BlockSpec((MIX_WIDTH, D_MODEL), lambda i: (0, 0)),
            pl.BlockSpec((tm, D_MODEL), lambda i: (i, 0)),
        ],
        out_specs=pl.BlockSpec((tm, D_MODEL), lambda i: (i, 0)),
        out_shape=jax.ShapeDtypeStruct((t, D_MODEL), _F32),
        compiler_params=pltpu.CompilerParams(
            dimension_semantics=("parallel",),
            vmem_limit_bytes=_VMEM_LIMIT_BYTES),
        name="outproj",
    )(oa, ob, w, x2d)


def _mlp_body(x_ref, gm_ref, wu_ref, wd_ref, gf_ref, o_ref, h_ref, acc_ref):
    j = pl.program_id(1)

    @pl.when(j == 0)
    def _():
        x = x_ref[...]
        h_ref[...] = (_rms_scale(x) * gm_ref[...]).astype(_BF16)
        acc_ref[...] = x

    u = jnp.maximum(jnp.dot(h_ref[...], wu_ref[...], preferred_element_type=_F32), 0.0)
    acc_ref[...] += jnp.dot((u * u).astype(_BF16), wd_ref[...], preferred_element_type=_F32)

    @pl.when(j == pl.num_programs(1) - 1)
    def _():
        o_ref[...] = _rms_scale(acc_ref[...]) * gf_ref[...]


def _mlp(x1, g_mlp, w_up, w_down, g_final, *, tm=512, tf=1024):
    t = x1.shape[0]
    return pl.pallas_call(
        _mlp_body,
        grid=(t // tm, D_FF // tf),
        in_specs=[
            pl.BlockSpec((tm, D_MODEL), lambda i, j: (i, 0)),
            pl.BlockSpec((1, D_MODEL), lambda i, j: (0, 0)),
            pl.BlockSpec((D_MODEL, tf), lambda i, j: (0, j)),
            pl.BlockSpec((tf, D_MODEL), lambda i, j: (j, 0)),
            pl.BlockSpec((1, D_MODEL), lambda i, j: (0, 0)),
        ],
        out_specs=pl.BlockSpec((tm, D_MODEL), lambda i, j: (i, 0)),
        out_shape=jax.ShapeDtypeStruct((t, D_MODEL), _F32),
        scratch_shapes=[pltpu.VMEM((tm, D_MODEL), _BF16), pltpu.VMEM((tm, D_MODEL), _F32)],
        compiler_params=pltpu.CompilerParams(
            dimension_semantics=("parallel", "arbitrary"),
            vmem_limit_bytes=_VMEM_LIMIT_BYTES),
        name="mlp",
    )(x1, g_mlp, w_up, w_down, g_final)


def _alibi_slopes():
    i = jnp.arange(1, N_HEADS_TOTAL + 1, dtype=_F32)
    s = jnp.exp2(-8.0 / N_HEADS_TOTAL * i)
    return s[0::2], s[1::2]


def _encoder(x, p):
    batch, seq = x.shape[0], x.shape[1]
    x2d = x.reshape(batch * seq, D_MODEL)
    proj = _inproj(x2d, p["g_attn"], p["w_in"], p["colscale"])
    oa = _winattn(proj, p["params_a"], batch=batch, seq=seq)
    ob = _diffattn(proj, p["params_b"], p["vectors_b"], batch=batch, seq=seq)
    x1 = _outproj(oa, ob, p["w_out"], x2d)
    y = _mlp(x1, p["g_mlp"], p["w_up"], p["w_down"], p["g_final"])
    return y.reshape(batch, seq, D_MODEL)


def kernel(x_prompt, x_sample, norm_attn_g, w_in, sink_logits, lambda_q1, lambda_k1, lambda_q2, lambda_k2,
           diff_subln_g, w_out, norm_mlp_g, w_up, w_down, norm_final_g):
    slopes_a, slopes_b = _alibi_slopes()
    col = jnp.arange(IN_WIDTH)
    in_bq = (col >= _BQ_BLK * HEAD_DIM) & (col < _BK_BLK * HEAD_DIM)
    vectors_b = jnp.zeros((32, HEAD_DIM), _F32)
    for r, vec in enumerate((lambda_q1, lambda_k1, lambda_q2, lambda_k2)):
        vectors_b = vectors_b.at[r, 0:B_QK_DIM].set(vec[0].astype(_F32))
    vectors_b = vectors_b.at[4, :].set(diff_subln_g[0].astype(_F32))
    p = {
        "g_attn": norm_attn_g[0].reshape(1, D_MODEL).astype(_F32),
        "w_in": w_in[0].astype(_BF16),
        "colscale": jnp.where(in_bq, 1.0 / math.sqrt(B_QK_DIM), 1.0).astype(_F32).reshape(1, IN_WIDTH),
        "params_a": jnp.concatenate([slopes_a, sink_logits[0].astype(_F32)]),
        "params_b": jnp.concatenate([slopes_b, 1.0 / slopes_b]),
        "vectors_b": vectors_b,
        "w_out": w_out[0].astype(_BF16),
        "g_mlp": norm_mlp_g[0].reshape(1, D_MODEL).astype(_F32),
        "w_up": w_up[0].astype(_BF16),
        "w_down": w_down[0].astype(_BF16),
        "g_final": norm_final_g.reshape(1, D_MODEL).astype(_F32),
    }
    return (_encoder(x_prompt, p), _encoder(x_sample, p))
```

```python
import functools
import math

import jax
import jax.numpy as jnp
from jax import lax
from jax.experimental import pallas as pl
from jax.experimental.pallas import tpu as pltpu

D_MODEL = 2048
HEAD_DIM = 128
N_HEADS_TOTAL = D_MODEL // HEAD_DIM
A_HEADS = N_HEADS_TOTAL // 2
A_KV_HEADS = 2
A_GROUP = A_HEADS // A_KV_HEADS
WINDOW = 128
BLOCK = 128
B_HEADS = N_HEADS_TOTAL - A_HEADS
B_QK_DIM = HEAD_DIM // 2
B_V_DIM = HEAD_DIM
MIX_WIDTH = A_HEADS * HEAD_DIM + B_HEADS * B_V_DIM
D_FF = 4 * D_MODEL
EPS = 1e-5
A_Q_W = A_HEADS * HEAD_DIM
A_KV_W = A_KV_HEADS * HEAD_DIM
B_QK_W = B_HEADS * 2 * B_QK_DIM
B_V_W = B_HEADS * B_V_DIM
IN_WIDTH = A_Q_W + 2 * A_KV_W + 2 * B_QK_W + B_V_W
LAMBDA_INIT = 0.8 - 0.6 * math.exp(-0.3 * 0)

_AK_BLK = A_Q_W // HEAD_DIM
_AV_BLK = (A_Q_W + A_KV_W) // HEAD_DIM
_BQ_BLK = (A_Q_W + 2 * A_KV_W) // HEAD_DIM
_BK_BLK = _BQ_BLK + B_QK_W // HEAD_DIM
_BV_BLK = _BK_BLK + B_QK_W // HEAD_DIM

_VMEM_LIMIT_BYTES = 56 * 1024 * 1024

_NT = (((1,), (1,)), ((), ()))

_BF16 = jnp.bfloat16
_F32 = jnp.float32
_LOG2E = math.log2(math.e)


def _rms_scale(x):
    return x * lax.rsqrt(jnp.mean(x * x, axis=-1, keepdims=True) + EPS)


def _inproj_body(x_ref, g_ref, w_ref, cs_ref, o_ref, h_ref):
    @pl.when(pl.program_id(1) == 0)
    def _():
        h_ref[...] = (_rms_scale(x_ref[...]) * g_ref[...]).astype(_BF16)

    acc = jnp.dot(h_ref[...], w_ref[...], preferred_element_type=_F32)
    o_ref[...] = (acc * cs_ref[...]).astype(o_ref.dtype)


def _inproj(x2d, g, w, colscale, *, tm=1024, tn=1536):
    t = x2d.shape[0]
    return pl.pallas_call(
        _inproj_body,
        grid=(t // tm, IN_WIDTH // tn),
        in_specs=[
            pl.BlockSpec((tm, D_MODEL), lambda i, j: (i, 0)),
            pl.BlockSpec((1, D_MODEL), lambda i, j: (0, 0)),
            pl.BlockSpec((D_MODEL, tn), lambda i, j: (0, j)),
            pl.BlockSpec((1, tn), lambda i, j: (0, j)),
        ],
        out_specs=pl.BlockSpec((tm, tn), lambda i, j: (i, j)),
        out_shape=jax.ShapeDtypeStruct((t, IN_WIDTH), _BF16),
        scratch_shapes=[pltpu.VMEM((tm, D_MODEL), _BF16)],
        compiler_params=pltpu.CompilerParams(
            dimension_semantics=("parallel", "arbitrary"),
            vmem_limit_bytes=_VMEM_LIMIT_BYTES),
        name="inproj",
    )(x2d, g, w, colscale)


_WIN_QBLOCKS = 8


def _winattn_body(par_ref, q_ref, kp_ref, kc_ref, kn_ref, vp_ref, vc_ref, vn_ref, o_ref, *, seq):
    i = pl.program_id(1)
    kv = pl.program_id(2)
    kall = jnp.concatenate([kp_ref[...], kc_ref[...], kn_ref[...]], axis=0)
    vall = jnp.concatenate([vp_ref[...], vc_ref[...], vn_ref[...]], axis=0)
    qi = lax.broadcasted_iota(jnp.int32, (BLOCK, 3 * BLOCK), 0)
    kj = lax.broadcasted_iota(jnp.int32, (BLOCK, 3 * BLOCK), 1)
    dist = jnp.abs(kj - BLOCK - qi)
    distf = dist.astype(_F32)
    in_window = dist <= WINDOW
    krow = lax.broadcasted_iota(jnp.int32, (1, 3 * BLOCK), 1)
    bias2, sink2 = [], []
    for g in range(A_GROUP):
        slope = par_ref[kv * A_GROUP + g]
        bias2.append(jnp.where(in_window, (-_LOG2E * slope) * distf, -jnp.inf))
        sink2.append(_LOG2E * par_ref[A_HEADS + kv * A_GROUP + g])

    def scores(blk):
        rows = slice(blk * BLOCK, (blk + 1) * BLOCK)
        q = jnp.concatenate([q_ref[rows, g * HEAD_DIM:(g + 1) * HEAD_DIM] for g in range(A_GROUP)], axis=0)
        return lax.dot_general(q, kall[blk * BLOCK:(blk + 3) * BLOCK], _NT, preferred_element_type=_F32)

    def softmax(blk, s):
        kpos = (i * _WIN_QBLOCKS + blk - 1) * BLOCK + krow
        edge2 = jnp.where((kpos >= 0) & (kpos < seq), 0.0, -jnp.inf)
        ps = []
        for g in range(A_GROUP):
            t = s[g * BLOCK:(g + 1) * BLOCK] * (_LOG2E / math.sqrt(HEAD_DIM)) + bias2[g] + edge2
            m = jnp.maximum(jnp.max(t, axis=-1, keepdims=True), sink2[g])
            p = jnp.exp2(t - m)
            den = jnp.sum(p, axis=-1, keepdims=True) + jnp.exp2(sink2[g] - m)
            ps.append((p * (1.0 / den)).astype(_BF16))
        return jnp.concatenate(ps, axis=0)

    def values(blk, p):
        rows = slice(blk * BLOCK, (blk + 1) * BLOCK)
        o = jnp.dot(p, vall[blk * BLOCK:(blk + 3) * BLOCK], preferred_element_type=_F32)
        for g in range(A_GROUP):
            o_ref[rows, g * HEAD_DIM:(g + 1) * HEAD_DIM] = o[g * BLOCK:(g + 1) * BLOCK].astype(o_ref.dtype)

    s_cur = scores(0)
    p_prev = None
    for blk in range(_WIN_QBLOCKS):
        s_next = scores(blk + 1) if blk + 1 < _WIN_QBLOCKS else None
        p_cur = softmax(blk, s_cur)
        if p_prev is not None:
            values(blk - 1, p_prev)
        s_cur, p_prev = s_next, p_cur
    values(_WIN_QBLOCKS - 1, p_prev)


def _winattn(proj, params_a, *, batch, seq):
    t = batch * seq
    nb = seq // BLOCK
    qb = _WIN_QBLOCKS
    assert nb % qb == 0
    nq = nb // qb
    qw = A_GROUP * HEAD_DIM

    def edge_spec(col0, first):
        def imap(b, i, kv):
            n = jnp.clip(i * qb - 1, 0, nb - 1) if first else jnp.clip((i + 1) * qb, 0, nb - 1)
            return (b * nb + n, col0 + kv)
        return pl.BlockSpec((BLOCK, HEAD_DIM), imap)

    def mid_spec(col0):
        return pl.BlockSpec((qb * BLOCK, HEAD_DIM), lambda b, i, kv: (b * nq + i, col0 + kv))

    return pl.pallas_call(
        functools.partial(_winattn_body, seq=seq),
        grid=(batch, nq, A_KV_HEADS),
        in_specs=[
            pl.BlockSpec(memory_space=pltpu.SMEM),
            pl.BlockSpec((qb * BLOCK, qw), lambda b, i, kv: (b * nq + i, kv)),
            edge_spec(_AK_BLK, True), mid_spec(_AK_BLK), edge_spec(_AK_BLK, False),
            edge_spec(_AV_BLK, True), mid_spec(_AV_BLK), edge_spec(_AV_BLK, False),
        ],
        out_specs=pl.BlockSpec((qb * BLOCK, qw), lambda b, i, kv: (b * nq + i, kv)),
        out_shape=jax.ShapeDtypeStruct((t, A_Q_W), _BF16),
        compiler_params=pltpu.CompilerParams(
            dimension_semantics=("parallel", "parallel", "parallel")),
        name="winattn",
    )(params_a, proj, proj, proj, proj, proj, proj, proj)


_ONES_ROWS = 16
_N_SPLIT = 3
_EXP_ZERO = 110.0
_NORM_SLACK = 2.1
_Q_SUBTILES = 4
_FIXED_MAX_SPAN = 60.0


def _split_bf16(t):
    pieces = []
    for _ in range(_N_SPLIT):
        piece = t.astype(_BF16)
        pieces.append(piece)
        t = t - piece.astype(_F32)
    return pieces


def _diffattn_body(par_ref, q_ref, k_ref, v_ref, lq1_ref, lk1_ref, lq2_ref, lk2_ref, g_ref, o_ref,
                   qm_ref, ka_ref, vt_ref, kn_ref, nb_ref, a_ref, p0_ref, p1_ref, m_ref, acc_ref, *, tq, tk, seq):
    h = pl.program_id(1)
    qi = pl.program_id(2)
    nk = seq // tk
    near = tq // tk
    slope = par_ref[h]
    inv_slope = par_ref[B_HEADS + h]
    lane_k = lax.broadcasted_iota(jnp.int32, (tk, HEAD_DIM), 1)

    def half_sq_norms_max(n2):
        return jnp.max(jnp.max(n2, axis=0, keepdims=True), axis=1, keepdims=True)

    def max_sq_norm(x):
        sq = x.astype(_F32) ** 2
        lane = lax.broadcasted_iota(jnp.int32, sq.shape, 1)
        s0 = jnp.sum(jnp.where(lane < B_QK_DIM, sq, 0.0), axis=1, keepdims=True)
        s1 = jnp.sum(jnp.where(lane >= B_QK_DIM, sq, 0.0), axis=1, keepdims=True)
        return half_sq_norms_max(jnp.maximum(s0, s1))

    def max_sq_norm_mxu(x):
        sq = (x.astype(_F32) ** 2).astype(_BF16)
        row = lax.broadcasted_iota(jnp.int32, (HEAD_DIM, HEAD_DIM), 0)
        colm = lax.broadcasted_iota(jnp.int32, (HEAD_DIM, HEAD_DIM), 1)
        half_sum = jnp.where((row // B_QK_DIM) == colm, 1.0, 0.0).astype(_BF16)
        return half_sq_norms_max(jnp.dot(sq, half_sum, preferred_element_type=_F32))

    @pl.when(qi == 0)
    def _():
        r = lax.broadcasted_iota(jnp.int32, (tk, HEAD_DIM), 0).astype(_F32)
        feats = []
        for mp in range(2):
            base = (1 - mp) * B_QK_DIM
            f = jnp.zeros((tk, HEAD_DIM), _F32)
            for c, piece in enumerate(_split_bf16(slope * r)):
                f = jnp.where(lane_k == base + c, piece.astype(_F32), f)
            feats.append(f)
        ones = jnp.ones((_ONES_ROWS, tk), _BF16)
        kn2 = jnp.zeros((1, 1), _F32)
        for t in range(nk):
            vt_ref[t] = jnp.concatenate([v_ref[t * tk:(t + 1) * tk, :].T, ones], axis=0)
            k = k_ref[t * tk:(t + 1) * tk, :]
            kn2 = jnp.maximum(kn2, max_sq_norm_mxu(k))
            ka_ref[0, t * tk:(t + 1) * tk, :] = jnp.where(lane_k < B_QK_DIM, k.astype(_F32), feats[0]).astype(_BF16)
            ka_ref[1, t * tk:(t + 1) * tk, :] = jnp.where(lane_k >= B_QK_DIM, k.astype(_F32), feats[1]).astype(_BF16)
        kn_ref[...] = kn2
        krow = lax.broadcasted_iota(jnp.int32, (tk, tq), 0)
        qcol = lax.broadcasted_iota(jnp.int32, (tk, tq), 1)
        for w in range(near):
            nb_ref[w] = slope * jnp.abs(qcol - krow - w * tk).astype(_F32)

    def query_tile(sub, carry):
        rows = pl.ds(pl.multiple_of(sub * tq, tq), tq)
        q = q_ref[rows, :]
        lane = lax.broadcasted_iota(jnp.int32, (tq, HEAD_DIM), 1)
        first_half = lane < B_QK_DIM
        for mp in range(2):
            own = first_half if mp == 0 else jnp.logical_not(first_half)
            base = (1 - mp) * B_QK_DIM
            sel = jnp.where((lane >= base) & (lane < base + _N_SPLIT), 1.0, 0.0)
            for kind, cols in enumerate((sel, -sel, jnp.zeros_like(sel))):
                qm_ref[kind, mp] = jnp.where(own, q.astype(_F32), cols).astype(_BF16)
        acc_ref[...] = jnp.zeros_like(acc_ref)

        col = lax.broadcasted_iota(jnp.int32, (1, 2 * tq), 1)
        q0 = (qi * _Q_SUBTILES + sub) * tq
        qpos = (q0 + jnp.where(col < tq, col, col - tq)).astype(_F32)

        kd = q0 // tk
        span = _NORM_SLACK * jnp.sqrt(max_sq_norm(q) * kn_ref[...])
        reach = (_EXP_ZERO + span) * inv_slope
        reach = jnp.where(reach < seq, reach, float(seq)).astype(jnp.int32)[0, 0] + 1
        lo = jnp.clip((q0 - reach) // tk, 0, kd)
        hi = jnp.clip((q0 + tq - 1 + reach) // tk, kd + near - 1, nk - 1)
        odd = (hi - lo + 1) % 2 == 1
        grow_hi = odd & (hi < nk - 1)
        hi = hi + grow_hi.astype(jnp.int32)
        lo = lo - (odd & jnp.logical_not(grow_hi)).astype(jnp.int32)
        n_tiles = hi - lo + 1
        n_far = n_tiles - near

        def far_index(t):
            ki = lo + t
            ki = ki + near * (ki >= kd).astype(jnp.int32)
            return ki, (ki > kd).astype(jnp.int32), pl.multiple_of(ki * tk, tk)

        def far_bias(t):
            _, after, k0 = far_index(t)
            sign = (1 - 2 * after).astype(_F32)
            return (sign * slope) * (k0.astype(_F32) - qpos)

        def tile_scores(k0, kind, mp):
            return lax.dot_general(ka_ref[mp, pl.ds(k0, tk), :], qm_ref[kind, mp], _NT, preferred_element_type=_F32)

        fixed_reference = jnp.where(span <= _FIXED_MAX_SPAN, 1, 0)[0, 0] == 1

        @pl.when(fixed_reference)
        def _():
            ref = 0.5 * span

            def probs(k0, kind, shift, p_ref):
                for mp in range(2):
                    cols = slice(mp * tq, (mp + 1) * tq)
                    sh = shift if shift.shape[0] == tk else shift[:, cols]
                    p_ref[:, cols] = jnp.exp(tile_scores(k0, kind, mp) - sh).astype(_BF16)

            def probs_at(pos, p_ref):
                if isinstance(pos, int) and pos < near:
                    probs(pl.multiple_of((kd + pos) * tk, tk), 2, nb_ref[pos] + ref, p_ref)
                else:
                    _, after, k0 = far_index(pos - near)
                    probs(k0, after, ref - far_bias(pos - near), p_ref)

            def values_at(pos, p_ref):
                ki = kd + pos if isinstance(pos, int) and pos < near else far_index(pos - near)[0]
                acc_ref[...] += jnp.dot(vt_ref[ki], p_ref[...], preferred_element_type=_F32)

            def pair(j):
                probs_at(2 * j + 1, p1_ref)
                values_at(2 * j, p0_ref)
                probs_at(2 * j + 2, p0_ref)
                values_at(2 * j + 1, p1_ref)

            probs_at(0, p0_ref)
            n_static = near // 2
            for j in range(n_static):
                pair(j)

            def grouped(count, first):
                def body(i, carry):
                    for u in range(count):
                        pair(first + count * i + u)
                    return carry
                return body

            n_pairs = n_tiles // 2 - 1
            done = jnp.int32(n_static)
            for count in (8, 4, 2, 1):
                trips = (n_pairs - done) // count
                lax.fori_loop(0, trips, grouped(count, done), 0)
                done = done + trips * count
            probs_at(n_tiles - 1, p1_ref)
            values_at(n_tiles - 2, p0_ref)
            values_at(n_tiles - 1, p1_ref)

        @pl.when(jnp.logical_not(fixed_reference))
        def _():
            m_ref[...] = jnp.full_like(m_ref, -jnp.inf)

            def fold(k0, kind, shift, c, ki):
                for mp in range(2):
                    cols = slice(mp * tq, (mp + 1) * tq)
                    a = tile_scores(k0, kind, mp)
                    a_ref[:, cols] = a if shift is None else a - shift
                a = a_ref[...]
                m_old = m_ref[...]
                m_new = jnp.maximum(m_old, jnp.max(a, axis=0, keepdims=True) + c)
                alpha = jnp.exp(m_old - m_new)
                p = jnp.exp(a - (m_new - c)).astype(_BF16)
                acc_ref[...] = alpha * acc_ref[...] + jnp.dot(vt_ref[ki], p, preferred_element_type=_F32)
                m_ref[...] = m_new

            for w in range(near):
                fold(pl.multiple_of((kd + w) * tk, tk), 2, nb_ref[w], jnp.zeros((1, 2 * tq), _F32), kd + w)

            def far(t, carry):
                ki, after, k0 = far_index(t)
                fold(k0, after, None, far_bias(t), ki)
                return carry

            lax.fori_loop(0, n_far, far, 0)

        lam = (jnp.exp(jnp.sum(lq1_ref[...] * lk1_ref[...], axis=-1, keepdims=True))
               - jnp.exp(jnp.sum(lq2_ref[...] * lk2_ref[...], axis=-1, keepdims=True))
               + LAMBDA_INIT)
        acc = acc_ref[...]
        on = acc[0:B_V_DIM] / acc[B_V_DIM:B_V_DIM + 1]
        o = on[:, 0:tq] - lam * on[:, tq:2 * tq]
        y = o * lax.rsqrt(jnp.mean(o * o, axis=0, keepdims=True) + EPS)
        o_ref[rows, :] = ((y.T * g_ref[...]) * (1.0 - LAMBDA_INIT)).astype(o_ref.dtype)
        return carry

    lax.fori_loop(0, _Q_SUBTILES, query_tile, 0)


def _diffattn(proj, params_b, lq1, lk1, lq2, lk2, subln_g, *, batch, seq, tq=512, tk=256):
    t = batch * seq
    tqs = tq * _Q_SUBTILES
    nq = seq // tqs
    nk = seq // tk
    near = tq // tk
    assert tq % tk == 0 and seq % tqs == 0 and nk % 2 == 0 and nk >= near + 2
    small = lambda shape: pl.BlockSpec(shape, lambda b, h, i: (0, 0))
    return pl.pallas_call(
        functools.partial(_diffattn_body, tq=tq, tk=tk, seq=seq),
        grid=(batch, B_HEADS, nq),
        in_specs=[
            pl.BlockSpec(memory_space=pltpu.SMEM),
            pl.BlockSpec((tqs, HEAD_DIM), lambda b, h, i: (b * nq + i, _BQ_BLK + h)),
            pl.BlockSpec((seq, HEAD_DIM), lambda b, h, i: (b, _BK_BLK + h)),
            pl.BlockSpec((seq, HEAD_DIM), lambda b, h, i: (b, _BV_BLK + h)),
            small((1, B_QK_DIM)), small((1, B_QK_DIM)), small((1, B_QK_DIM)), small((1, B_QK_DIM)),
            small((1, B_V_DIM)),
        ],
        out_specs=pl.BlockSpec((tqs, B_V_DIM), lambda b, h, i: (b * nq + i, h)),
        out_shape=jax.ShapeDtypeStruct((t, B_V_W), _BF16),
        scratch_shapes=[
            pltpu.VMEM((3, 2, tq, HEAD_DIM), _BF16),
            pltpu.VMEM((2, seq, HEAD_DIM), _BF16),
            pltpu.VMEM((nk, B_V_DIM + _ONES_ROWS, tk), _BF16),
            pltpu.VMEM((1, 1), _F32),
            pltpu.VMEM((near, tk, tq), _F32),
            pltpu.VMEM((tk, 2 * tq), _F32),
            pltpu.VMEM((tk, 2 * tq), _BF16),
            pltpu.VMEM((tk, 2 * tq), _BF16),
            pltpu.VMEM((1, 2 * tq), _F32),
            pltpu.VMEM((B_V_DIM + _ONES_ROWS, 2 * tq), _F32),
        ],
        compiler_params=pltpu.CompilerParams(
            dimension_semantics=("parallel", "parallel", "arbitrary"),
            vmem_limit_bytes=_VMEM_LIMIT_BYTES),
        name="diffattn",
    )(params_b, proj, proj, proj, lq1, lk1, lq2, lk2, subln_g)


def _outproj_body(oa_ref, ob_ref, w_ref, x_ref, o_ref):
    a = jnp.concatenate([oa_ref[...], ob_ref[...]], axis=1)
    o_ref[...] = x_ref[...] + jnp.dot(a, w_ref[...], preferred_element_type=_F32)


def _outproj(oa, ob, w, x2d, *, tm=512):
    t = x2d.shape[0]
    return pl.pallas_call(
        _outproj_body,
        grid=(t // tm,),
        in_specs=[
            pl.BlockSpec((tm, A_Q_W), lambda i: (i, 0)),
            pl.BlockSpec((tm, B_V_W), lambda i: (i, 0)),
            pl.BlockSpec((MIX_WIDTH, D_MODEL), lambda i: (0, 0)),
            pl.BlockSpec((tm, D_MODEL), lambda i: (i, 0)),
        ],
        out_specs=pl.BlockSpec((tm, D_MODEL), lambda i: (i, 0)),
        out_shape=jax.ShapeDtypeStruct((t, D_MODEL), _F32),
        compiler_params=pltpu.CompilerParams(
            dimension_semantics=("parallel",),
            vmem_limit_bytes=_VMEM_LIMIT_BYTES),
        name="outproj",
    )(oa, ob, w, x2d)


def _mlp_body(x_ref, gm_ref, wu_ref, wd_ref, gf_ref, o_ref, h_ref, acc_ref):
    j = pl.program_id(1)

    @pl.when(j == 0)
    def _():
        x = x_ref[...]
        h_ref[...] = (_rms_scale(x) * gm_ref[...]).astype(_BF16)
        acc_ref[...] = x

    u = jnp.maximum(jnp.dot(h_ref[...], wu_ref[...], preferred_element_type=_F32), 0.0)
    acc_ref[...] += jnp.dot((u * u).astype(_BF16), wd_ref[...], preferred_element_type=_F32)

    @pl.when(j == pl.num_programs(1) - 1)
    def _():
        o_ref[...] = _rms_scale(acc_ref[...]) * gf_ref[...]


def _mlp(x1, g_mlp, w_up, w_down, g_final, *, tm=512, tf=1024):
    t = x1.shape[0]
    return pl.pallas_call(
        _mlp_body,
        grid=(t // tm, D_FF // tf),
        in_specs=[
            pl.BlockSpec((tm, D_MODEL), lambda i, j: (i, 0)),
            pl.BlockSpec((1, D_MODEL), lambda i, j: (0, 0)),
            pl.BlockSpec((D_MODEL, tf), lambda i, j: (0, j)),
            pl.BlockSpec((tf, D_MODEL), lambda i, j: (j, 0)),
            pl.BlockSpec((1, D_MODEL), lambda i, j: (0, 0)),
        ],
        out_specs=pl.BlockSpec((tm, D_MODEL), lambda i, j: (i, 0)),
        out_shape=jax.ShapeDtypeStruct((t, D_MODEL), _F32),
        scratch_shapes=[pltpu.VMEM((tm, D_MODEL), _BF16), pltpu.VMEM((tm, D_MODEL), _F32)],
        compiler_params=pltpu.CompilerParams(
            dimension_semantics=("parallel", "arbitrary"),
            vmem_limit_bytes=_VMEM_LIMIT_BYTES),
        name="mlp",
    )(x1, g_mlp, w_up, w_down, g_final)


def _alibi_slopes():
    i = jnp.arange(1, N_HEADS_TOTAL + 1, dtype=_F32)
    s = jnp.exp2(-8.0 / N_HEADS_TOTAL * i)
    return s[0::2], s[1::2]


def _encoder(x, p):
    batch, seq = x.shape[0], x.shape[1]
    x2d = x.reshape(batch * seq, D_MODEL)
    proj = _inproj(x2d, p["g_attn"], p["w_in"], p["colscale"])
    oa = _winattn(proj, p["params_a"], batch=batch, seq=seq)
    ob = _diffattn(proj, p["params_b"], p["lq1"], p["lk1"], p["lq2"], p["lk2"], p["subln_g"],
                   batch=batch, seq=seq)
    x1 = _outproj(oa, ob, p["w_out"], x2d)
    y = _mlp(x1, p["g_mlp"], p["w_up"], p["w_down"], p["g_final"])
    return y.reshape(batch, seq, D_MODEL)


def kernel(x_prompt, x_sample, norm_attn_g, w_in, sink_logits, lambda_q1, lambda_k1, lambda_q2, lambda_k2,
           diff_subln_g, w_out, norm_mlp_g, w_up, w_down, norm_final_g):
    slopes_a, slopes_b = _alibi_slopes()
    col = jnp.arange(IN_WIDTH)
    in_bq = (col >= _BQ_BLK * HEAD_DIM) & (col < _BK_BLK * HEAD_DIM)
    p = {
        "g_attn": norm_attn_g[0].reshape(1, D_MODEL).astype(_F32),
        "w_in": w_in[0].astype(_BF16),
        "colscale": jnp.where(in_bq, 1.0 / math.sqrt(B_QK_DIM), 1.0).astype(_F32).reshape(1, IN_WIDTH),
        "params_a": jnp.concatenate([slopes_a, sink_logits[0].astype(_F32)]),
        "params_b": jnp.concatenate([slopes_b, 1.0 / slopes_b]),
        "lq1": lambda_q1[0].reshape(1, B_QK_DIM).astype(_F32),
        "lk1": lambda_k1[0].reshape(1, B_QK_DIM).astype(_F32),
        "lq2": lambda_q2[0].reshape(1, B_QK_DIM).astype(_F32),
        "lk2": lambda_k2[0].reshape(1, B_QK_DIM).astype(_F32),
        "subln_g": diff_subln_g[0].reshape(1, B_V_DIM).astype(_F32),
        "w_out": w_out[0].astype(_BF16),
        "g_mlp": norm_mlp_g[0].reshape(1, D_MODEL).astype(_F32),
        "w_up": w_up[0].astype(_BF16),
        "w_down": w_down[0].astype(_BF16),
        "g_final": norm_final_g.reshape(1, D_MODEL).astype(_F32),
    }
    return (_encoder(x_prompt, p), _encoder(x_sample, p))
```

```python
import functools
import math

import jax
import jax.numpy as jnp
from jax import lax
from jax.experimental import pallas as pl
from jax.experimental.pallas import tpu as pltpu

D_MODEL = 2048
HEAD_DIM = 128
N_HEADS_TOTAL = D_MODEL // HEAD_DIM
A_HEADS = N_HEADS_TOTAL // 2
A_KV_HEADS = 2
A_GROUP = A_HEADS // A_KV_HEADS
WINDOW = 128
BLOCK = 128
B_HEADS = N_HEADS_TOTAL - A_HEADS
B_QK_DIM = HEAD_DIM // 2
B_V_DIM = HEAD_DIM
MIX_WIDTH = A_HEADS * HEAD_DIM + B_HEADS * B_V_DIM
D_FF = 4 * D_MODEL
EPS = 1e-5
A_Q_W = A_HEADS * HEAD_DIM
A_KV_W = A_KV_HEADS * HEAD_DIM
B_QK_W = B_HEADS * 2 * B_QK_DIM
B_V_W = B_HEADS * B_V_DIM
IN_WIDTH = A_Q_W + 2 * A_KV_W + 2 * B_QK_W + B_V_W
LAMBDA_INIT = 0.8 - 0.6 * math.exp(-0.3 * 0)

_AK_BLK = A_Q_W // HEAD_DIM
_AV_BLK = (A_Q_W + A_KV_W) // HEAD_DIM
_BQ_BLK = (A_Q_W + 2 * A_KV_W) // HEAD_DIM
_BK_BLK = _BQ_BLK + B_QK_W // HEAD_DIM
_BV_BLK = _BK_BLK + B_QK_W // HEAD_DIM

_VMEM_LIMIT_BYTES = 56 * 1024 * 1024

_NT = (((1,), (1,)), ((), ()))

_BF16 = jnp.bfloat16
_F32 = jnp.float32
_LOG2E = math.log2(math.e)


def _rms_scale(x):
    return x * lax.rsqrt(jnp.mean(x * x, axis=-1, keepdims=True) + EPS)


def _inproj_body(x_ref, g_ref, w_ref, cs_ref, o_ref, h_ref):
    @pl.when(pl.program_id(1) == 0)
    def _():
        h_ref[...] = (_rms_scale(x_ref[...]) * g_ref[...]).astype(_BF16)

    acc = jnp.dot(h_ref[...], w_ref[...], preferred_element_type=_F32)
    o_ref[...] = (acc * cs_ref[...]).astype(o_ref.dtype)


def _inproj(x2d, g, w, colscale, *, tm=1024, tn=1536):
    t = x2d.shape[0]
    return pl.pallas_call(
        _inproj_body,
        grid=(t // tm, IN_WIDTH // tn),
        in_specs=[
            pl.BlockSpec((tm, D_MODEL), lambda i, j: (i, 0)),
            pl.BlockSpec((1, D_MODEL), lambda i, j: (0, 0)),
            pl.BlockSpec((D_MODEL, tn), lambda i, j: (0, j)),
            pl.BlockSpec((1, tn), lambda i, j: (0, j)),
        ],
        out_specs=pl.BlockSpec((tm, tn), lambda i, j: (i, j)),
        out_shape=jax.ShapeDtypeStruct((t, IN_WIDTH), _BF16),
        scratch_shapes=[pltpu.VMEM((tm, D_MODEL), _BF16)],
        compiler_params=pltpu.CompilerParams(
            dimension_semantics=("parallel", "arbitrary"),
            vmem_limit_bytes=_VMEM_LIMIT_BYTES),
        name="inproj",
    )(x2d, g, w, colscale)


_WIN_QBLOCKS = 8


def _winattn_body(par_ref, q_ref, kp_ref, kc_ref, kn_ref, vp_ref, vc_ref, vn_ref, o_ref, *, seq):
    i = pl.program_id(1)
    kv = pl.program_id(2)
    kall = jnp.concatenate([kp_ref[...], kc_ref[...], kn_ref[...]], axis=0)
    vall = jnp.concatenate([vp_ref[...], vc_ref[...], vn_ref[...]], axis=0)
    qi = lax.broadcasted_iota(jnp.int32, (BLOCK, 3 * BLOCK), 0)
    kj = lax.broadcasted_iota(jnp.int32, (BLOCK, 3 * BLOCK), 1)
    dist = jnp.abs(kj - BLOCK - qi)
    distf = dist.astype(_F32)
    in_window = dist <= WINDOW
    krow = lax.broadcasted_iota(jnp.int32, (1, 3 * BLOCK), 1)
    bias2, sink2 = [], []
    for g in range(A_GROUP):
        slope = par_ref[kv * A_GROUP + g]
        bias2.append(jnp.where(in_window, (-_LOG2E * slope) * distf, -jnp.inf))
        sink2.append(_LOG2E * par_ref[A_HEADS + kv * A_GROUP + g])

    def scores(blk):
        rows = slice(blk * BLOCK, (blk + 1) * BLOCK)
        q = jnp.concatenate([q_ref[rows, g * HEAD_DIM:(g + 1) * HEAD_DIM] for g in range(A_GROUP)], axis=0)
        return lax.dot_general(q, kall[blk * BLOCK:(blk + 3) * BLOCK], _NT, preferred_element_type=_F32)

    def softmax(blk, s):
        kpos = (i * _WIN_QBLOCKS + blk - 1) * BLOCK + krow
        edge2 = jnp.where((kpos >= 0) & (kpos < seq), 0.0, -jnp.inf)
        ps = []
        for g in range(A_GROUP):
            t = s[g * BLOCK:(g + 1) * BLOCK] * (_LOG2E / math.sqrt(HEAD_DIM)) + bias2[g] + edge2
            m = jnp.maximum(jnp.max(t, axis=-1, keepdims=True), sink2[g])
            p = jnp.exp2(t - m)
            den = jnp.sum(p, axis=-1, keepdims=True) + jnp.exp2(sink2[g] - m)
            ps.append((p * (1.0 / den)).astype(_BF16))
        return jnp.concatenate(ps, axis=0)

    def values(blk, p):
        rows = slice(blk * BLOCK, (blk + 1) * BLOCK)
        o = jnp.dot(p, vall[blk * BLOCK:(blk + 3) * BLOCK], preferred_element_type=_F32)
        for g in range(A_GROUP):
            o_ref[rows, g * HEAD_DIM:(g + 1) * HEAD_DIM] = o[g * BLOCK:(g + 1) * BLOCK].astype(o_ref.dtype)

    s_cur = scores(0)
    p_prev = None
    for blk in range(_WIN_QBLOCKS):
        s_next = scores(blk + 1) if blk + 1 < _WIN_QBLOCKS else None
        p_cur = softmax(blk, s_cur)
        if p_prev is not None:
            values(blk - 1, p_prev)
        s_cur, p_prev = s_next, p_cur
    values(_WIN_QBLOCKS - 1, p_prev)


def _winattn(proj, params_a, *, batch, seq):
    t = batch * seq
    nb = seq // BLOCK
    qb = _WIN_QBLOCKS
    assert nb % qb == 0
    nq = nb // qb
    qw = A_GROUP * HEAD_DIM

    def edge_spec(col0, first):
        def imap(b, i, kv):
            n = jnp.clip(i * qb - 1, 0, nb - 1) if first else jnp.clip((i + 1) * qb, 0, nb - 1)
            return (b * nb + n, col0 + kv)
        return pl.BlockSpec((BLOCK, HEAD_DIM), imap)

    def mid_spec(col0):
        return pl.BlockSpec((qb * BLOCK, HEAD_DIM), lambda b, i, kv: (b * nq + i, col0 + kv))

    return pl.pallas_call(
        functools.partial(_winattn_body, seq=seq),
        grid=(batch, nq, A_KV_HEADS),
        in_specs=[
            pl.BlockSpec(memory_space=pltpu.SMEM),
            pl.BlockSpec((qb * BLOCK, qw), lambda b, i, kv: (b * nq + i, kv)),
            edge_spec(_AK_BLK, True), mid_spec(_AK_BLK), edge_spec(_AK_BLK, False),
            edge_spec(_AV_BLK, True), mid_spec(_AV_BLK), edge_spec(_AV_BLK, False),
        ],
        out_specs=pl.BlockSpec((qb * BLOCK, qw), lambda b, i, kv: (b * nq + i, kv)),
        out_shape=jax.ShapeDtypeStruct((t, A_Q_W), _BF16),
        compiler_params=pltpu.CompilerParams(
            dimension_semantics=("parallel", "parallel", "parallel")),
        name="winattn",
    )(params_a, proj, proj, proj, proj, proj, proj, proj)


_ONES_ROWS = 16
_N_SPLIT = 3
_EXP_ZERO = 110.0
_NORM_SLACK = 2.1
_Q_SUBTILES = 4
_FIXED_MAX_SPAN = 60.0


def _split_bf16(t):
    pieces = []
    for _ in range(_N_SPLIT):
        piece = t.astype(_BF16)
        pieces.append(piece)
        t = t - piece.astype(_F32)
    return pieces


def _diffattn_body(par_ref, q_ref, k_ref, v_ref, lq1_ref, lk1_ref, lq2_ref, lk2_ref, g_ref, o_ref,
                   qm_ref, ka_ref, vt_ref, kn_ref, nb_ref, a_ref, p0_ref, p1_ref, m_ref, acc_ref, *, tq, tk, seq):
    h = pl.program_id(1)
    qi = pl.program_id(2)
    nk = seq // tk
    near = tq // tk
    slope = par_ref[h]
    inv_slope = par_ref[B_HEADS + h]
    lane_k = lax.broadcasted_iota(jnp.int32, (tk, HEAD_DIM), 1)

    def half_sq_norms_max(n2):
        return jnp.max(jnp.max(n2, axis=0, keepdims=True), axis=1, keepdims=True)

    def max_sq_norm(x):
        sq = x.astype(_F32) ** 2
        lane = lax.broadcasted_iota(jnp.int32, sq.shape, 1)
        s0 = jnp.sum(jnp.where(lane < B_QK_DIM, sq, 0.0), axis=1, keepdims=True)
        s1 = jnp.sum(jnp.where(lane >= B_QK_DIM, sq, 0.0), axis=1, keepdims=True)
        return half_sq_norms_max(jnp.maximum(s0, s1))

    def max_sq_norm_mxu(x):
        sq = (x.astype(_F32) ** 2).astype(_BF16)
        row = lax.broadcasted_iota(jnp.int32, (HEAD_DIM, HEAD_DIM), 0)
        colm = lax.broadcasted_iota(jnp.int32, (HEAD_DIM, HEAD_DIM), 1)
        half_sum = jnp.where((row // B_QK_DIM) == colm, 1.0, 0.0).astype(_BF16)
        return half_sq_norms_max(jnp.dot(sq, half_sum, preferred_element_type=_F32))

    @pl.when(qi == 0)
    def _():
        r = lax.broadcasted_iota(jnp.int32, (tk, HEAD_DIM), 0).astype(_F32)
        feats = []
        for mp in range(2):
            base = (1 - mp) * B_QK_DIM
            f = jnp.zeros((tk, HEAD_DIM), _F32)
            for c, piece in enumerate(_split_bf16(slope * r)):
                f = jnp.where(lane_k == base + c, piece.astype(_F32), f)
            feats.append(f)
        ones = jnp.ones((_ONES_ROWS, tk), _BF16)
        kn2 = jnp.zeros((1, 1), _F32)
        for t in range(nk):
            vt_ref[t] = jnp.concatenate([v_ref[t * tk:(t + 1) * tk, :].T, ones], axis=0)
            k = k_ref[t * tk:(t + 1) * tk, :]
            kn2 = jnp.maximum(kn2, max_sq_norm_mxu(k))
            ka_ref[0, t * tk:(t + 1) * tk, :] = jnp.where(lane_k < B_QK_DIM, k.astype(_F32), feats[0]).astype(_BF16)
            ka_ref[1, t * tk:(t + 1) * tk, :] = jnp.where(lane_k >= B_QK_DIM, k.astype(_F32), feats[1]).astype(_BF16)
        kn_ref[...] = kn2
        krow = lax.broadcasted_iota(jnp.int32, (tk, tq), 0)
        qcol = lax.broadcasted_iota(jnp.int32, (tk, tq), 1)
        for w in range(near):
            nb_ref[w] = slope * jnp.abs(qcol - krow - w * tk).astype(_F32)

    def query_tile(sub, carry):
        rows = pl.ds(pl.multiple_of(sub * tq, tq), tq)
        q = q_ref[rows, :]
        lane = lax.broadcasted_iota(jnp.int32, (tq, HEAD_DIM), 1)
        first_half = lane < B_QK_DIM
        for mp in range(2):
            own = first_half if mp == 0 else jnp.logical_not(first_half)
            base = (1 - mp) * B_QK_DIM
            sel = jnp.where((lane >= base) & (lane < base + _N_SPLIT), 1.0, 0.0)
            for kind, cols in enumerate((sel, -sel, jnp.zeros_like(sel))):
                qm_ref[kind, mp] = jnp.where(own, q.astype(_F32), cols).astype(_BF16)
        acc_ref[...] = jnp.zeros_like(acc_ref)

        col = lax.broadcasted_iota(jnp.int32, (1, 2 * tq), 1)
        q0 = (qi * _Q_SUBTILES + sub) * tq
        qpos = (q0 + jnp.where(col < tq, col, col - tq)).astype(_F32)

        kd = q0 // tk
        span = _NORM_SLACK * jnp.sqrt(max_sq_norm(q) * kn_ref[...])
        reach = (_EXP_ZERO + span) * inv_slope
        reach = jnp.where(reach < seq, reach, float(seq)).astype(jnp.int32)[0, 0] + 1
        lo = jnp.clip((q0 - reach) // tk, 0, kd)
        hi = jnp.clip((q0 + tq - 1 + reach) // tk, kd + near - 1, nk - 1)
        odd = (hi - lo + 1) % 2 == 1
        grow_hi = odd & (hi < nk - 1)
        hi = hi + grow_hi.astype(jnp.int32)
        lo = lo - (odd & jnp.logical_not(grow_hi)).astype(jnp.int32)
        n_tiles = hi - lo + 1
        n_far = n_tiles - near

        def far_index(t):
            ki = lo + t
            ki = ki + near * (ki >= kd).astype(jnp.int32)
            return ki, (ki > kd).astype(jnp.int32), pl.multiple_of(ki * tk, tk)

        def far_bias(t):
            _, after, k0 = far_index(t)
            sign = (1 - 2 * after).astype(_F32)
            return (sign * slope) * (k0.astype(_F32) - qpos)

        def tile_scores(k0, kind, mp):
            return lax.dot_general(ka_ref[mp, pl.ds(k0, tk), :], qm_ref[kind, mp], _NT, preferred_element_type=_F32)

        fixed_reference = jnp.where(span <= _FIXED_MAX_SPAN, 1, 0)[0, 0] == 1

        @pl.when(fixed_reference)
        def _():
            ref = 0.5 * span

            def probs(k0, kind, shift, p_ref):
                for mp in range(2):
                    cols = slice(mp * tq, (mp + 1) * tq)
                    sh = shift if shift.shape[0] == tk else shift[:, cols]
                    p_ref[:, cols] = jnp.exp(tile_scores(k0, kind, mp) - sh).astype(_BF16)

            def probs_at(pos, p_ref):
                if isinstance(pos, int) and pos < near:
                    probs(pl.multiple_of((kd + pos) * tk, tk), 2, nb_ref[pos] + ref, p_ref)
                else:
                    _, after, k0 = far_index(pos - near)
                    probs(k0, after, ref - far_bias(pos - near), p_ref)

            def values_at(pos, p_ref):
                ki = kd + pos if isinstance(pos, int) and pos < near else far_index(pos - near)[0]
                acc_ref[...] += jnp.dot(vt_ref[ki], p_ref[...], preferred_element_type=_F32)

            def pair(j):
                probs_at(2 * j + 1, p1_ref)
                values_at(2 * j, p0_ref)
                probs_at(2 * j + 2, p0_ref)
                values_at(2 * j + 1, p1_ref)

            probs_at(0, p0_ref)
            n_static = near // 2
            for j in range(n_static):
                pair(j)

            def grouped(count, first):
                def body(i, carry):
                    for u in range(count):
                        pair(first + count * i + u)
                    return carry
                return body

            n_pairs = n_tiles // 2 - 1
            done = jnp.int32(n_static)
            for count in (8, 4, 2, 1):
                trips = (n_pairs - done) // count
                lax.fori_loop(0, trips, grouped(count, done), 0)
                done = done + trips * count
            probs_at(n_tiles - 1, p1_ref)
            values_at(n_tiles - 2, p0_ref)
            values_at(n_tiles - 1, p1_ref)

        @pl.when(jnp.logical_not(fixed_reference))
        def _():
            m_ref[...] = jnp.full_like(m_ref, -jnp.inf)

            def fold(k0, kind, shift, c, ki):
                for mp in range(2):
                    cols = slice(mp * tq, (mp + 1) * tq)
                    a = tile_scores(k0, kind, mp)
                    a_ref[:, cols] = a if shift is None else a - shift
                a = a_ref[...]
                m_old = m_ref[...]
                m_new = jnp.maximum(m_old, jnp.max(a, axis=0, keepdims=True) + c)
                alpha = jnp.exp(m_old - m_new)
                p = jnp.exp(a - (m_new - c)).astype(_BF16)
                acc_ref[...] = alpha * acc_ref[...] + jnp.dot(vt_ref[ki], p, preferred_element_type=_F32)
                m_ref[...] = m_new

            for w in range(near):
                fold(pl.multiple_of((kd + w) * tk, tk), 2, nb_ref[w], jnp.zeros((1, 2 * tq), _F32), kd + w)

            def far(t, carry):
                ki, after, k0 = far_index(t)
                fold(k0, after, None, far_bias(t), ki)
                return carry

            lax.fori_loop(0, n_far, far, 0)

        lam = (jnp.exp(jnp.sum(lq1_ref[...] * lk1_ref[...], axis=-1, keepdims=True))
               - jnp.exp(jnp.sum(lq2_ref[...] * lk2_ref[...], axis=-1, keepdims=True))
               + LAMBDA_INIT)
        acc = acc_ref[...]
        on = acc[0:B_V_DIM] / acc[B_V_DIM:B_V_DIM + 1]
        o = on[:, 0:tq] - lam * on[:, tq:2 * tq]
        y = o * lax.rsqrt(jnp.mean(o * o, axis=0, keepdims=True) + EPS)
        o_ref[rows, :] = ((y.T * g_ref[...]) * (1.0 - LAMBDA_INIT)).astype(o_ref.dtype)
        return carry

    lax.fori_loop(0, _Q_SUBTILES, query_tile, 0)


def _diffattn(proj, params_b, lq1, lk1, lq2, lk2, subln_g, *, batch, seq, tq=1024, tk=256):
    t = batch * seq
    tqs = tq * _Q_SUBTILES
    nq = seq // tqs
    nk = seq // tk
    near = tq // tk
    assert tq % tk == 0 and seq % tqs == 0 and nk % 2 == 0 and nk >= near + 2
    small = lambda shape: pl.BlockSpec(shape, lambda b, h, i: (0, 0))
    return pl.pallas_call(
        functools.partial(_diffattn_body, tq=tq, tk=tk, seq=seq),
        grid=(batch, B_HEADS, nq),
        in_specs=[
            pl.BlockSpec(memory_space=pltpu.SMEM),
            pl.BlockSpec((tqs, HEAD_DIM), lambda b, h, i: (b * nq + i, _BQ_BLK + h)),
            pl.BlockSpec((seq, HEAD_DIM), lambda b, h, i: (b, _BK_BLK + h)),
            pl.BlockSpec((seq, HEAD_DIM), lambda b, h, i: (b, _BV_BLK + h)),
            small((1, B_QK_DIM)), small((1, B_QK_DIM)), small((1, B_QK_DIM)), small((1, B_QK_DIM)),
            small((1, B_V_DIM)),
        ],
        out_specs=pl.BlockSpec((tqs, B_V_DIM), lambda b, h, i: (b * nq + i, h)),
        out_shape=jax.ShapeDtypeStruct((t, B_V_W), _BF16),
        scratch_shapes=[
            pltpu.VMEM((3, 2, tq, HEAD_DIM), _BF16),
            pltpu.VMEM((2, seq, HEAD_DIM), _BF16),
            pltpu.VMEM((nk, B_V_DIM + _ONES_ROWS, tk), _BF16),
            pltpu.VMEM((1, 1), _F32),
            pltpu.VMEM((near, tk, tq), _F32),
            pltpu.VMEM((tk, 2 * tq), _F32),
            pltpu.VMEM((tk, 2 * tq), _BF16),
            pltpu.VMEM((tk, 2 * tq), _BF16),
            pltpu.VMEM((1, 2 * tq), _F32),
            pltpu.VMEM((B_V_DIM + _ONES_ROWS, 2 * tq), _F32),
        ],
        compiler_params=pltpu.CompilerParams(
            dimension_semantics=("parallel", "parallel", "arbitrary"),
            vmem_limit_bytes=_VMEM_LIMIT_BYTES),
        name="diffattn",
    )(params_b, proj, proj, proj, lq1, lk1, lq2, lk2, subln_g)


def _outproj_body(oa_ref, ob_ref, w_ref, x_ref, o_ref):
    a = jnp.concatenate([oa_ref[...], ob_ref[...]], axis=1)
    o_ref[...] = x_ref[...] + jnp.dot(a, w_ref[...], preferred_element_type=_F32)


def _outproj(oa, ob, w, x2d, *, tm=512):
    t = x2d.shape[0]
    return pl.pallas_call(
        _outproj_body,
        grid=(t // tm,),
        in_specs=[
            pl.BlockSpec((tm, A_Q_W), lambda i: (i, 0)),
            pl.BlockSpec((tm, B_V_W), lambda i: (i, 0)),
            pl.BlockSpec((MIX_WIDTH, D_MODEL), lambda i: (0, 0)),
            pl.BlockSpec((tm, D_MODEL), lambda i: (i, 0)),
        ],
        out_specs=pl.BlockSpec((tm, D_MODEL), lambda i: (i, 0)),
        out_shape=jax.ShapeDtypeStruct((t, D_MODEL), _F32),
        compiler_params=pltpu.CompilerParams(
            dimension_semantics=("parallel",),
            vmem_limit_bytes=_VMEM_LIMIT_BYTES),
        name="outproj",
    )(oa, ob, w, x2d)


def _mlp_body(x_ref, gm_ref, wu_ref, wd_ref, gf_ref, o_ref, h_ref, acc_ref):
    j = pl.program_id(1)

    @pl.when(j == 0)
    def _():
        x = x_ref[...]
        h_ref[...] = (_rms_scale(x) * gm_ref[...]).astype(_BF16)
        acc_ref[...] = x

    u = jnp.maximum(jnp.dot(h_ref[...], wu_ref[...], preferred_element_type=_F32), 0.0)
    acc_ref[...] += jnp.dot((u * u).astype(_BF16), wd_ref[...], preferred_element_type=_F32)

    @pl.when(j == pl.num_programs(1) - 1)
    def _():
        o_ref[...] = _rms_scale(acc_ref[...]) * gf_ref[...]


def _mlp(x1, g_mlp, w_up, w_down, g_final, *, tm=512, tf=1024):
    t = x1.shape[0]
    return pl.pallas_call(
        _mlp_body,
        grid=(t // tm, D_FF // tf),
        in_specs=[
            pl.BlockSpec((tm, D_MODEL), lambda i, j: (i, 0)),
            pl.BlockSpec((1, D_MODEL), lambda i, j: (0, 0)),
            pl.BlockSpec((D_MODEL, tf), lambda i, j: (0, j)),
            pl.BlockSpec((tf, D_MODEL), lambda i, j: (j, 0)),
            pl.BlockSpec((1, D_MODEL), lambda i, j: (0, 0)),
        ],
        out_specs=pl.BlockSpec((tm, D_MODEL), lambda i, j: (i, 0)),
        out_shape=jax.ShapeDtypeStruct((t, D_MODEL), _F32),
        scratch_shapes=[pltpu.VMEM((tm, D_MODEL), _BF16), pltpu.VMEM((tm, D_MODEL), _F32)],
        compiler_params=pltpu.CompilerParams(
            dimension_semantics=("parallel", "arbitrary"),
            vmem_limit_bytes=_VMEM_LIMIT_BYTES),
        name="mlp",
    )(x1, g_mlp, w_up, w_down, g_final)


def _alibi_slopes():
    i = jnp.arange(1, N_HEADS_TOTAL + 1, dtype=_F32)
    s = jnp.exp2(-8.0 / N_HEADS_TOTAL * i)
    return s[0::2], s[1::2]


def _encoder(x, p):
    batch, seq = x.shape[0], x.shape[1]
    x2d = x.reshape(batch * seq, D_MODEL)
    proj = _inproj(x2d, p["g_attn"], p["w_in"], p["colscale"])
    oa = _winattn(proj, p["params_a"], batch=batch, seq=seq)
    ob = _diffattn(proj, p["params_b"], p["lq1"], p["lk1"], p["lq2"], p["lk2"], p["subln_g"],
                   batch=batch, seq=seq)
    x1 = _outproj(oa, ob, p["w_out"], x2d)
    y = _mlp(x1, p["g_mlp"], p["w_up"], p["w_down"], p["g_final"])
    return y.reshape(batch, seq, D_MODEL)


def kernel(x_prompt, x_sample, norm_attn_g, w_in, sink_logits, lambda_q1, lambda_k1, lambda_q2, lambda_k2,
           diff_subln_g, w_out, norm_mlp_g, w_up, w_down, norm_final_g):
    slopes_a, slopes_b = _alibi_slopes()
    col = jnp.arange(IN_WIDTH)
    in_bq = (col >= _BQ_BLK * HEAD_DIM) & (col < _BK_BLK * HEAD_DIM)
    p = {
        "g_attn": norm_attn_g[0].reshape(1, D_MODEL).astype(_F32),
        "w_in": w_in[0].astype(_BF16),
        "colscale": jnp.where(in_bq, 1.0 / math.sqrt(B_QK_DIM), 1.0).astype(_F32).reshape(1, IN_WIDTH),
        "params_a": jnp.concatenate([slopes_a, sink_logits[0].astype(_F32)]),
        "params_b": jnp.concatenate([slopes_b, 1.0 / slopes_b]),
        "lq1": lambda_q1[0].reshape(1, B_QK_DIM).astype(_F32),
        "lk1": lambda_k1[0].reshape(1, B_QK_DIM).astype(_F32),
        "lq2": lambda_q2[0].reshape(1, B_QK_DIM).astype(_F32),
        "lk2": lambda_k2[0].reshape(1, B_QK_DIM).astype(_F32),
        "subln_g": diff_subln_g[0].reshape(1, B_V_DIM).astype(_F32),
        "w_out": w_out[0].astype(_BF16),
        "g_mlp": norm_mlp_g[0].reshape(1, D_MODEL).astype(_F32),
        "w_up": w_up[0].astype(_BF16),
        "w_down": w_down[0].astype(_BF16),
        "g_final": norm_final_g.reshape(1, D_MODEL).astype(_F32),
    }
    return (_encoder(x_prompt, p), _encoder(x_sample, p))
```

```python
import functools
import math

import jax
import jax.numpy as jnp
from jax import lax
from jax.experimental import pallas as pl
from jax.experimental.pallas import tpu as pltpu

D_MODEL = 2048
HEAD_DIM = 128
N_HEADS_TOTAL = D_MODEL // HEAD_DIM
A_HEADS = N_HEADS_TOTAL // 2
A_KV_HEADS = 2
A_GROUP = A_HEADS // A_KV_HEADS
WINDOW = 128
BLOCK = 128
B_HEADS = N_HEADS_TOTAL - A_HEADS
B_QK_DIM = HEAD_DIM // 2
B_V_DIM = HEAD_DIM
MIX_WIDTH = A_HEADS * HEAD_DIM + B_HEADS * B_V_DIM
D_FF = 4 * D_MODEL
EPS = 1e-5
A_Q_W = A_HEADS * HEAD_DIM
A_KV_W = A_KV_HEADS * HEAD_DIM
B_QK_W = B_HEADS * 2 * B_QK_DIM
B_V_W = B_HEADS * B_V_DIM
IN_WIDTH = A_Q_W + 2 * A_KV_W + 2 * B_QK_W + B_V_W
LAMBDA_INIT = 0.8 - 0.6 * math.exp(-0.3 * 0)

_AK_BLK = A_Q_W // HEAD_DIM
_AV_BLK = (A_Q_W + A_KV_W) // HEAD_DIM
_BQ_BLK = (A_Q_W + 2 * A_KV_W) // HEAD_DIM
_BK_BLK = _BQ_BLK + B_QK_W // HEAD_DIM
_BV_BLK = _BK_BLK + B_QK_W // HEAD_DIM

_VMEM_LIMIT_BYTES = 56 * 1024 * 1024

_NT = (((1,), (1,)), ((), ()))

_BF16 = jnp.bfloat16
_F32 = jnp.float32
_LOG2E = math.log2(math.e)


def _rms_scale(x):
    return x * lax.rsqrt(jnp.mean(x * x, axis=-1, keepdims=True) + EPS)


def _inproj_body(x_ref, g_ref, w_ref, cs_ref, o_ref, h_ref):
    @pl.when(pl.program_id(1) == 0)
    def _():
        h_ref[...] = (_rms_scale(x_ref[...]) * g_ref[...]).astype(_BF16)

    acc = jnp.dot(h_ref[...], w_ref[...], preferred_element_type=_F32)
    o_ref[...] = (acc * cs_ref[...]).astype(o_ref.dtype)


def _inproj(x2d, g, w, colscale, *, tm=1024, tn=1536):
    t = x2d.shape[0]
    return pl.pallas_call(
        _inproj_body,
        grid=(t // tm, IN_WIDTH // tn),
        in_specs=[
            pl.BlockSpec((tm, D_MODEL), lambda i, j: (i, 0)),
            pl.BlockSpec((1, D_MODEL), lambda i, j: (0, 0)),
            pl.BlockSpec((D_MODEL, tn), lambda i, j: (0, j)),
            pl.BlockSpec((1, tn), lambda i, j: (0, j)),
        ],
        out_specs=pl.BlockSpec((tm, tn), lambda i, j: (i, j)),
        out_shape=jax.ShapeDtypeStruct((t, IN_WIDTH), _BF16),
        scratch_shapes=[pltpu.VMEM((tm, D_MODEL), _BF16)],
        compiler_params=pltpu.CompilerParams(
            dimension_semantics=("parallel", "arbitrary"),
            vmem_limit_bytes=_VMEM_LIMIT_BYTES),
        name="inproj",
    )(x2d, g, w, colscale)


_WIN_QBLOCKS = 8


def _winattn_body(par_ref, q_ref, kp_ref, kc_ref, kn_ref, vp_ref, vc_ref, vn_ref, o_ref, *, seq):
    i = pl.program_id(1)
    kv = pl.program_id(2)
    kall = jnp.concatenate([kp_ref[...], kc_ref[...], kn_ref[...]], axis=0)
    vall = jnp.concatenate([vp_ref[...], vc_ref[...], vn_ref[...]], axis=0)
    qi = lax.broadcasted_iota(jnp.int32, (BLOCK, 3 * BLOCK), 0)
    kj = lax.broadcasted_iota(jnp.int32, (BLOCK, 3 * BLOCK), 1)
    dist = jnp.abs(kj - BLOCK - qi)
    distf = dist.astype(_F32)
    in_window = dist <= WINDOW
    krow = lax.broadcasted_iota(jnp.int32, (1, 3 * BLOCK), 1)
    bias2, sink2 = [], []
    for g in range(A_GROUP):
        slope = par_ref[kv * A_GROUP + g]
        bias2.append(jnp.where(in_window, (-_LOG2E * slope) * distf, -jnp.inf))
        sink2.append(_LOG2E * par_ref[A_HEADS + kv * A_GROUP + g])

    def scores(blk):
        rows = slice(blk * BLOCK, (blk + 1) * BLOCK)
        q = jnp.concatenate([q_ref[rows, g * HEAD_DIM:(g + 1) * HEAD_DIM] for g in range(A_GROUP)], axis=0)
        return lax.dot_general(q, kall[blk * BLOCK:(blk + 3) * BLOCK], _NT, preferred_element_type=_F32)

    def softmax(blk, s):
        kpos = (i * _WIN_QBLOCKS + blk - 1) * BLOCK + krow
        edge2 = jnp.where((kpos >= 0) & (kpos < seq), 0.0, -jnp.inf)
        ps = []
        for g in range(A_GROUP):
            t = s[g * BLOCK:(g + 1) * BLOCK] * (_LOG2E / math.sqrt(HEAD_DIM)) + bias2[g] + edge2
            m = jnp.maximum(jnp.max(t, axis=-1, keepdims=True), sink2[g])
            p = jnp.exp2(t - m)
            den = jnp.sum(p, axis=-1, keepdims=True) + jnp.exp2(sink2[g] - m)
            ps.append((p * (1.0 / den)).astype(_BF16))
        return jnp.concatenate(ps, axis=0)

    def values(blk, p):
        rows = slice(blk * BLOCK, (blk + 1) * BLOCK)
        o = jnp.dot(p, vall[blk * BLOCK:(blk + 3) * BLOCK], preferred_element_type=_F32)
        for g in range(A_GROUP):
            o_ref[rows, g * HEAD_DIM:(g + 1) * HEAD_DIM] = o[g * BLOCK:(g + 1) * BLOCK].astype(o_ref.dtype)

    s_cur = scores(0)
    p_prev = None
    for blk in range(_WIN_QBLOCKS):
        s_next = scores(blk + 1) if blk + 1 < _WIN_QBLOCKS else None
        p_cur = softmax(blk, s_cur)
        if p_prev is not None:
            values(blk - 1, p_prev)
        s_cur, p_prev = s_next, p_cur
    values(_WIN_QBLOCKS - 1, p_prev)


def _winattn(proj, params_a, *, batch, seq):
    t = batch * seq
    nb = seq // BLOCK
    qb = _WIN_QBLOCKS
    assert nb % qb == 0
    nq = nb // qb
    qw = A_GROUP * HEAD_DIM

    def edge_spec(col0, first):
        def imap(b, i, kv):
            n = jnp.clip(i * qb - 1, 0, nb - 1) if first else jnp.clip((i + 1) * qb, 0, nb - 1)
            return (b * nb + n, col0 + kv)
        return pl.BlockSpec((BLOCK, HEAD_DIM), imap)

    def mid_spec(col0):
        return pl.BlockSpec((qb * BLOCK, HEAD_DIM), lambda b, i, kv: (b * nq + i, col0 + kv))

    return pl.pallas_call(
        functools.partial(_winattn_body, seq=seq),
        grid=(batch, nq, A_KV_HEADS),
        in_specs=[
            pl.BlockSpec(memory_space=pltpu.SMEM),
            pl.BlockSpec((qb * BLOCK, qw), lambda b, i, kv: (b * nq + i, kv)),
            edge_spec(_AK_BLK, True), mid_spec(_AK_BLK), edge_spec(_AK_BLK, False),
            edge_spec(_AV_BLK, True), mid_spec(_AV_BLK), edge_spec(_AV_BLK, False),
        ],
        out_specs=pl.BlockSpec((qb * BLOCK, qw), lambda b, i, kv: (b * nq + i, kv)),
        out_shape=jax.ShapeDtypeStruct((t, A_Q_W), _BF16),
        compiler_params=pltpu.CompilerParams(
            dimension_semantics=("parallel", "parallel", "parallel")),
        name="winattn",
    )(params_a, proj, proj, proj, proj, proj, proj, proj)


_ONES_ROWS = 16
_N_SPLIT = 3
_EXP_ZERO = 110.0
_NORM_SLACK = 2.1
_Q_SUBTILES = 4
_FIXED_MAX_SPAN = 60.0


def _split_bf16(t):
    pieces = []
    for _ in range(_N_SPLIT):
        piece = t.astype(_BF16)
        pieces.append(piece)
        t = t - piece.astype(_F32)
    return pieces


def _diffattn_body(par_ref, q_ref, k_ref, v_ref, lq1_ref, lk1_ref, lq2_ref, lk2_ref, g_ref, o_ref,
                   qm_ref, ka_ref, vt_ref, kn_ref, nb_ref, a_ref, p0_ref, p1_ref, m_ref, acc_ref, *, tq, tk, seq):
    h = pl.program_id(1)
    qi = pl.program_id(2)
    nk = seq // tk
    near = tq // tk
    slope = par_ref[h]
    inv_slope = par_ref[B_HEADS + h]
    lane_k = lax.broadcasted_iota(jnp.int32, (tk, HEAD_DIM), 1)

    def half_sq_norms_max(n2):
        return jnp.max(jnp.max(n2, axis=0, keepdims=True), axis=1, keepdims=True)

    def max_sq_norm(x):
        sq = x.astype(_F32) ** 2
        lane = lax.broadcasted_iota(jnp.int32, sq.shape, 1)
        s0 = jnp.sum(jnp.where(lane < B_QK_DIM, sq, 0.0), axis=1, keepdims=True)
        s1 = jnp.sum(jnp.where(lane >= B_QK_DIM, sq, 0.0), axis=1, keepdims=True)
        return half_sq_norms_max(jnp.maximum(s0, s1))

    def max_sq_norm_mxu(x):
        sq = (x.astype(_F32) ** 2).astype(_BF16)
        row = lax.broadcasted_iota(jnp.int32, (HEAD_DIM, HEAD_DIM), 0)
        colm = lax.broadcasted_iota(jnp.int32, (HEAD_DIM, HEAD_DIM), 1)
        half_sum = jnp.where((row // B_QK_DIM) == colm, 1.0, 0.0).astype(_BF16)
        return half_sq_norms_max(jnp.dot(sq, half_sum, preferred_element_type=_F32))

    @pl.when(qi == 0)
    def _():
        r = lax.broadcasted_iota(jnp.int32, (tk, HEAD_DIM), 0).astype(_F32)
        feats = []
        for mp in range(2):
            base = (1 - mp) * B_QK_DIM
            f = jnp.zeros((tk, HEAD_DIM), _F32)
            for c, piece in enumerate(_split_bf16(slope * r)):
                f = jnp.where(lane_k == base + c, piece.astype(_F32), f)
            feats.append(f)
        ones = jnp.ones((_ONES_ROWS, tk), _BF16)
        kn2 = jnp.zeros((1, 1), _F32)
        for t in range(nk):
            vt_ref[t] = jnp.concatenate([v_ref[t * tk:(t + 1) * tk, :].T, ones], axis=0)
            k = k_ref[t * tk:(t + 1) * tk, :]
            kn2 = jnp.maximum(kn2, max_sq_norm_mxu(k))
            ka_ref[0, t * tk:(t + 1) * tk, :] = jnp.where(lane_k < B_QK_DIM, k.astype(_F32), feats[0]).astype(_BF16)
            ka_ref[1, t * tk:(t + 1) * tk, :] = jnp.where(lane_k >= B_QK_DIM, k.astype(_F32), feats[1]).astype(_BF16)
        kn_ref[...] = kn2
        krow = lax.broadcasted_iota(jnp.int32, (tk, tq), 0)
        qcol = lax.broadcasted_iota(jnp.int32, (tk, tq), 1)
        for w in range(near):
            nb_ref[w] = slope * jnp.abs(qcol - krow - w * tk).astype(_F32)

    def query_tile(sub, carry):
        rows = pl.ds(pl.multiple_of(sub * tq, tq), tq)
        q = q_ref[rows, :]
        lane = lax.broadcasted_iota(jnp.int32, (tq, HEAD_DIM), 1)
        first_half = lane < B_QK_DIM
        for mp in range(2):
            own = first_half if mp == 0 else jnp.logical_not(first_half)
            base = (1 - mp) * B_QK_DIM
            sel = jnp.where((lane >= base) & (lane < base + _N_SPLIT), 1.0, 0.0)
            for kind, cols in enumerate((sel, -sel, jnp.zeros_like(sel))):
                qm_ref[kind, mp] = jnp.where(own, q.astype(_F32), cols).astype(_BF16)
        acc_ref[...] = jnp.zeros_like(acc_ref)

        col = lax.broadcasted_iota(jnp.int32, (1, 2 * tq), 1)
        q0 = (qi * _Q_SUBTILES + sub) * tq
        qpos = (q0 + jnp.where(col < tq, col, col - tq)).astype(_F32)

        kd = q0 // tk
        span = _NORM_SLACK * jnp.sqrt(max_sq_norm(q) * kn_ref[...])
        reach = (_EXP_ZERO + span) * inv_slope
        reach = jnp.where(reach < seq, reach, float(seq)).astype(jnp.int32)[0, 0] + 1
        lo = jnp.clip((q0 - reach) // tk, 0, kd)
        hi = jnp.clip((q0 + tq - 1 + reach) // tk, kd + near - 1, nk - 1)
        odd = (hi - lo + 1) % 2 == 1
        grow_hi = odd & (hi < nk - 1)
        hi = hi + grow_hi.astype(jnp.int32)
        lo = lo - (odd & jnp.logical_not(grow_hi)).astype(jnp.int32)
        n_tiles = hi - lo + 1
        n_far = n_tiles - near

        def far_index(t):
            ki = lo + t
            ki = ki + near * (ki >= kd).astype(jnp.int32)
            return ki, (ki > kd).astype(jnp.int32), pl.multiple_of(ki * tk, tk)

        def far_bias(t):
            _, after, k0 = far_index(t)
            sign = (1 - 2 * after).astype(_F32)
            return (sign * slope) * (k0.astype(_F32) - qpos)

        def tile_scores(k0, kind, mp):
            return lax.dot_general(ka_ref[mp, pl.ds(k0, tk), :], qm_ref[kind, mp], _NT, preferred_element_type=_F32)

        fixed_reference = jnp.where(span <= _FIXED_MAX_SPAN, 1, 0)[0, 0] == 1

        @pl.when(fixed_reference)
        def _():
            ref = 0.5 * span

            def probs(k0, kind, shift, p_ref):
                for mp in range(2):
                    cols = slice(mp * tq, (mp + 1) * tq)
                    sh = shift if shift.shape[0] == tk else shift[:, cols]
                    p_ref[:, cols] = jnp.exp(tile_scores(k0, kind, mp) - sh).astype(_BF16)

            def probs_at(pos, p_ref):
                if isinstance(pos, int) and pos < near:
                    probs(pl.multiple_of((kd + pos) * tk, tk), 2, nb_ref[pos] + ref, p_ref)
                else:
                    _, after, k0 = far_index(pos - near)
                    probs(k0, after, ref - far_bias(pos - near), p_ref)

            def values_at(pos, p_ref):
                ki = kd + pos if isinstance(pos, int) and pos < near else far_index(pos - near)[0]
                acc_ref[...] += jnp.dot(vt_ref[ki], p_ref[...], preferred_element_type=_F32)

            def pair(j):
                probs_at(2 * j + 1, p1_ref)
                values_at(2 * j, p0_ref)
                probs_at(2 * j + 2, p0_ref)
                values_at(2 * j + 1, p1_ref)

            probs_at(0, p0_ref)
            n_static = near // 2
            for j in range(n_static):
                pair(j)

            def grouped(count, first):
                def body(i, carry):
                    for u in range(count):
                        pair(first + count * i + u)
                    return carry
                return body

            n_pairs = n_tiles // 2 - 1
            done = jnp.int32(n_static)
            for count in (8, 4, 2, 1):
                trips = (n_pairs - done) // count
                lax.fori_loop(0, trips, grouped(count, done), 0)
                done = done + trips * count
            probs_at(n_tiles - 1, p1_ref)
            values_at(n_tiles - 2, p0_ref)
            values_at(n_tiles - 1, p1_ref)

        @pl.when(jnp.logical_not(fixed_reference))
        def _():
            m_ref[...] = jnp.full_like(m_ref, -jnp.inf)

            def fold(k0, kind, shift, c, ki):
                for mp in range(2):
                    cols = slice(mp * tq, (mp + 1) * tq)
                    a = tile_scores(k0, kind, mp)
                    a_ref[:, cols] = a if shift is None else a - shift
                a = a_ref[...]
                m_old = m_ref[...]
                m_new = jnp.maximum(m_old, jnp.max(a, axis=0, keepdims=True) + c)
                alpha = jnp.exp(m_old - m_new)
                p = jnp.exp(a - (m_new - c)).astype(_BF16)
                acc_ref[...] = alpha * acc_ref[...] + jnp.dot(vt_ref[ki], p, preferred_element_type=_F32)
                m_ref[...] = m_new

            for w in range(near):
                fold(pl.multiple_of((kd + w) * tk, tk), 2, nb_ref[w], jnp.zeros((1, 2 * tq), _F32), kd + w)

            def far(t, carry):
                ki, after, k0 = far_index(t)
                fold(k0, after, None, far_bias(t), ki)
                return carry

            lax.fori_loop(0, n_far, far, 0)

        lam = (jnp.exp(jnp.sum(lq1_ref[...] * lk1_ref[...], axis=-1, keepdims=True))
               - jnp.exp(jnp.sum(lq2_ref[...] * lk2_ref[...], axis=-1, keepdims=True))
               + LAMBDA_INIT)
        acc = acc_ref[...]
        on = acc[0:B_V_DIM] / acc[B_V_DIM:B_V_DIM + 1]
        o = on[:, 0:tq] - lam * on[:, tq:2 * tq]
        y = o * lax.rsqrt(jnp.mean(o * o, axis=0, keepdims=True) + EPS)
        o_ref[rows, :] = ((y.T * g_ref[...]) * (1.0 - LAMBDA_INIT)).astype(o_ref.dtype)
        return carry

    lax.fori_loop(0, _Q_SUBTILES, query_tile, 0)


def _diffattn(proj, params_b, lq1, lk1, lq2, lk2, subln_g, *, batch, seq, tq=1024, tk=256):
    t = batch * seq
    tqs = tq * _Q_SUBTILES
    nq = seq // tqs
    nk = seq // tk
    near = tq // tk
    assert tq % tk == 0 and seq % tqs == 0 and nk % 2 == 0 and nk >= near + 2
    small = lambda shape: pl.BlockSpec(shape, lambda b, h, i: (0, 0))
    return pl.pallas_call(
        functools.partial(_diffattn_body, tq=tq, tk=tk, seq=seq),
        grid=(batch, B_HEADS, nq),
        in_specs=[
            pl.BlockSpec(memory_space=pltpu.SMEM),
            pl.BlockSpec((tqs, HEAD_DIM), lambda b, h, i: (b * nq + i, _BQ_BLK + h)),
            pl.BlockSpec((seq, HEAD_DIM), lambda b, h, i: (b, _BK_BLK + h)),
            pl.BlockSpec((seq, HEAD_DIM), lambda b, h, i: (b, _BV_BLK + h)),
            small((1, B_QK_DIM)), small((1, B_QK_DIM)), small((1, B_QK_DIM)), small((1, B_QK_DIM)),
            small((1, B_V_DIM)),
        ],
        out_specs=pl.BlockSpec((tqs, B_V_DIM), lambda b, h, i: (b * nq + i, h)),
        out_shape=jax.ShapeDtypeStruct((t, B_V_W), _BF16),
        scratch_shapes=[
            pltpu.VMEM((3, 2, tq, HEAD_DIM), _BF16),
            pltpu.VMEM((2, seq, HEAD_DIM), _BF16),
            pltpu.VMEM((nk, B_V_DIM + _ONES_ROWS, tk), _BF16),
            pltpu.VMEM((1, 1), _F32),
            pltpu.VMEM((near, tk, tq), _F32),
            pltpu.VMEM((tk, 2 * tq), _F32),
            pltpu.VMEM((tk, 2 * tq), _BF16),
            pltpu.VMEM((tk, 2 * tq), _BF16),
            pltpu.VMEM((1, 2 * tq), _F32),
            pltpu.VMEM((B_V_DIM + _ONES_ROWS, 2 * tq), _F32),
        ],
        compiler_params=pltpu.CompilerParams(
            dimension_semantics=("parallel", "parallel", "arbitrary"),
            vmem_limit_bytes=_VMEM_LIMIT_BYTES),
        name="diffattn",
    )(params_b, proj, proj, proj, lq1, lk1, lq2, lk2, subln_g)


def _mlp_body(oa_ref, ob_ref, wo_ref, x_ref, gm_ref, wu_ref, wd_ref, gf_ref, o_ref, h_ref, acc_ref):
    j = pl.program_id(1)

    @pl.when(j == 0)
    def _():
        a = jnp.concatenate([oa_ref[...], ob_ref[...]], axis=1)
        x1 = x_ref[...] + jnp.dot(a, wo_ref[...], preferred_element_type=_F32)
        h_ref[...] = (_rms_scale(x1) * gm_ref[...]).astype(_BF16)
        acc_ref[...] = x1

    u = jnp.maximum(jnp.dot(h_ref[...], wu_ref[...], preferred_element_type=_F32), 0.0)
    acc_ref[...] += jnp.dot((u * u).astype(_BF16), wd_ref[...], preferred_element_type=_F32)

    @pl.when(j == pl.num_programs(1) - 1)
    def _():
        o_ref[...] = _rms_scale(acc_ref[...]) * gf_ref[...]


def _mlp(oa, ob, w_out, x2d, g_mlp, w_up, w_down, g_final, *, tm=512, tf=1024):
    t = x2d.shape[0]
    return pl.pallas_call(
        _mlp_body,
        grid=(t // tm, D_FF // tf),
        in_specs=[
            pl.BlockSpec((tm, A_Q_W), lambda i, j: (i, 0)),
            pl.BlockSpec((tm, B_V_W), lambda i, j: (i, 0)),
            pl.BlockSpec((MIX_WIDTH, D_MODEL), lambda i, j: (0, 0), pipeline_mode=pl.Buffered(1)),
            pl.BlockSpec((tm, D_MODEL), lambda i, j: (i, 0)),
            pl.BlockSpec((1, D_MODEL), lambda i, j: (0, 0)),
            pl.BlockSpec((D_MODEL, tf), lambda i, j: (0, j)),
            pl.BlockSpec((tf, D_MODEL), lambda i, j: (j, 0)),
            pl.BlockSpec((1, D_MODEL), lambda i, j: (0, 0)),
        ],
        out_specs=pl.BlockSpec((tm, D_MODEL), lambda i, j: (i, 0)),
        out_shape=jax.ShapeDtypeStruct((t, D_MODEL), _F32),
        scratch_shapes=[pltpu.VMEM((tm, D_MODEL), _BF16), pltpu.VMEM((tm, D_MODEL), _F32)],
        compiler_params=pltpu.CompilerParams(
            dimension_semantics=("parallel", "arbitrary"),
            vmem_limit_bytes=_VMEM_LIMIT_BYTES),
        name="mlp",
    )(oa, ob, w_out, x2d, g_mlp, w_up, w_down, g_final)


def _alibi_slopes():
    i = jnp.arange(1, N_HEADS_TOTAL + 1, dtype=_F32)
    s = jnp.exp2(-8.0 / N_HEADS_TOTAL * i)
    return s[0::2], s[1::2]


def _encoder(x, p):
    batch, seq = x.shape[0], x.shape[1]
    x2d = x.reshape(batch * seq, D_MODEL)
    proj = _inproj(x2d, p["g_attn"], p["w_in"], p["colscale"])
    oa = _winattn(proj, p["params_a"], batch=batch, seq=seq)
    ob = _diffattn(proj, p["params_b"], p["lq1"], p["lk1"], p["lq2"], p["lk2"], p["subln_g"],
                   batch=batch, seq=seq)
    y = _mlp(oa, ob, p["w_out"], x2d, p["g_mlp"], p["w_up"], p["w_down"], p["g_final"])
    return y.reshape(batch, seq, D_MODEL)


def kernel(x_prompt, x_sample, norm_attn_g, w_in, sink_logits, lambda_q1, lambda_k1, lambda_q2, lambda_k2,
           diff_subln_g, w_out, norm_mlp_g, w_up, w_down, norm_final_g):
    slopes_a, slopes_b = _alibi_slopes()
    col = jnp.arange(IN_WIDTH)
    in_bq = (col >= _BQ_BLK * HEAD_DIM) & (col < _BK_BLK * HEAD_DIM)
    p = {
        "g_attn": norm_attn_g[0].reshape(1, D_MODEL).astype(_F32),
        "w_in": w_in[0].astype(_BF16),
        "colscale": jnp.where(in_bq, 1.0 / math.sqrt(B_QK_DIM), 1.0).astype(_F32).reshape(1, IN_WIDTH),
        "params_a": jnp.concatenate([slopes_a, sink_logits[0].astype(_F32)]),
        "params_b": jnp.concatenate([slopes_b, 1.0 / slopes_b]),
        "lq1": lambda_q1[0].reshape(1, B_QK_DIM).astype(_F32),
        "lk1": lambda_k1[0].reshape(1, B_QK_DIM).astype(_F32),
        "lq2": lambda_q2[0].reshape(1, B_QK_DIM).astype(_F32),
        "lk2": lambda_k2[0].reshape(1, B_QK_DIM).astype(_F32),
        "subln_g": diff_subln_g[0].reshape(1, B_V_DIM).astype(_F32),
        "w_out": w_out[0].astype(_BF16),
        "g_mlp": norm_mlp_g[0].reshape(1, D_MODEL).astype(_F32),
        "w_up": w_up[0].astype(_BF16),
        "w_down": w_down[0].astype(_BF16),
        "g_final": norm_final_g.reshape(1, D_MODEL).astype(_F32),
    }
    return (_encoder(x_prompt, p), _encoder(x_sample, p))
```

```python
import functools
import math

import jax
import jax.numpy as jnp
from jax import lax
from jax.experimental import pallas as pl
from jax.experimental.pallas import tpu as pltpu

D_MODEL = 2048
HEAD_DIM = 128
N_HEADS_TOTAL = D_MODEL // HEAD_DIM
A_HEADS = N_HEADS_TOTAL // 2
A_KV_HEADS = 2
A_GROUP = A_HEADS // A_KV_HEADS
WINDOW = 128
BLOCK = 128
B_HEADS = N_HEADS_TOTAL - A_HEADS
B_QK_DIM = HEAD_DIM // 2
B_V_DIM = HEAD_DIM
MIX_WIDTH = A_HEADS * HEAD_DIM + B_HEADS * B_V_DIM
D_FF = 4 * D_MODEL
EPS = 1e-5
A_Q_W = A_HEADS * HEAD_DIM
A_KV_W = A_KV_HEADS * HEAD_DIM
B_QK_W = B_HEADS * 2 * B_QK_DIM
B_V_W = B_HEADS * B_V_DIM
IN_WIDTH = A_Q_W + 2 * A_KV_W + 2 * B_QK_W + B_V_W
LAMBDA_INIT = 0.8 - 0.6 * math.exp(-0.3 * 0)

_AK_BLK = A_Q_W // HEAD_DIM
_AV_BLK = (A_Q_W + A_KV_W) // HEAD_DIM
_BQ_BLK = (A_Q_W + 2 * A_KV_W) // HEAD_DIM
_BK_BLK = _BQ_BLK + B_QK_W // HEAD_DIM
_BV_BLK = _BK_BLK + B_QK_W // HEAD_DIM

_VMEM_LIMIT_BYTES = 56 * 1024 * 1024

_NT = (((1,), (1,)), ((), ()))

_BF16 = jnp.bfloat16
_F32 = jnp.float32
_LOG2E = math.log2(math.e)


def _rms_scale(x):
    return x * lax.rsqrt(jnp.mean(x * x, axis=-1, keepdims=True) + EPS)


def _inproj_body(x_ref, g_ref, w_ref, cs_ref, o_ref, h_ref):
    @pl.when(pl.program_id(1) == 0)
    def _():
        h_ref[...] = (_rms_scale(x_ref[...]) * g_ref[...]).astype(_BF16)

    acc = jnp.dot(h_ref[...], w_ref[...], preferred_element_type=_F32)
    o_ref[...] = (acc * cs_ref[...]).astype(o_ref.dtype)


def _inproj(x2d, g, w, colscale, *, tm=1024, tn=1536):
    t = x2d.shape[0]
    return pl.pallas_call(
        _inproj_body,
        grid=(t // tm, IN_WIDTH // tn),
        in_specs=[
            pl.BlockSpec((tm, D_MODEL), lambda i, j: (i, 0)),
            pl.BlockSpec((1, D_MODEL), lambda i, j: (0, 0)),
            pl.BlockSpec((D_MODEL, tn), lambda i, j: (0, j)),
            pl.BlockSpec((1, tn), lambda i, j: (0, j)),
        ],
        out_specs=pl.BlockSpec((tm, tn), lambda i, j: (i, j)),
        out_shape=jax.ShapeDtypeStruct((t, IN_WIDTH), _BF16),
        scratch_shapes=[pltpu.VMEM((tm, D_MODEL), _BF16)],
        compiler_params=pltpu.CompilerParams(
            dimension_semantics=("parallel", "arbitrary"),
            vmem_limit_bytes=_VMEM_LIMIT_BYTES),
        name="inproj",
    )(x2d, g, w, colscale)


_WIN_QBLOCKS = 16


def _winattn_body(par_ref, q_ref, kp_ref, kc_ref, kn_ref, vp_ref, vc_ref, vn_ref, o_ref, *, seq):
    i = pl.program_id(1)
    kv = pl.program_id(2)
    kall = jnp.concatenate([kp_ref[...], kc_ref[...], kn_ref[...]], axis=0)
    vall = jnp.concatenate([vp_ref[...], vc_ref[...], vn_ref[...]], axis=0)
    qi = lax.broadcasted_iota(jnp.int32, (BLOCK, 3 * BLOCK), 0)
    kj = lax.broadcasted_iota(jnp.int32, (BLOCK, 3 * BLOCK), 1)
    dist = jnp.abs(kj - BLOCK - qi)
    distf = dist.astype(_F32)
    in_window = dist <= WINDOW
    krow = lax.broadcasted_iota(jnp.int32, (1, 3 * BLOCK), 1)
    bias2, sink2 = [], []
    for g in range(A_GROUP):
        slope = par_ref[kv * A_GROUP + g]
        bias2.append(jnp.where(in_window, (-_LOG2E * slope) * distf, -jnp.inf))
        sink2.append(_LOG2E * par_ref[A_HEADS + kv * A_GROUP + g])

    def scores(blk):
        rows = slice(blk * BLOCK, (blk + 1) * BLOCK)
        q = jnp.concatenate([q_ref[rows, g * HEAD_DIM:(g + 1) * HEAD_DIM] for g in range(A_GROUP)], axis=0)
        return lax.dot_general(q, kall[blk * BLOCK:(blk + 3) * BLOCK], _NT, preferred_element_type=_F32)

    def softmax(blk, s):
        kpos = (i * _WIN_QBLOCKS + blk - 1) * BLOCK + krow
        edge2 = jnp.where((kpos >= 0) & (kpos < seq), 0.0, -jnp.inf)
        ps = []
        for g in range(A_GROUP):
            t = s[g * BLOCK:(g + 1) * BLOCK] * (_LOG2E / math.sqrt(HEAD_DIM)) + bias2[g] + edge2
            m = jnp.maximum(jnp.max(t, axis=-1, keepdims=True), sink2[g])
            p = jnp.exp2(t - m)
            den = jnp.sum(p, axis=-1, keepdims=True) + jnp.exp2(sink2[g] - m)
            ps.append((p * (1.0 / den)).astype(_BF16))
        return jnp.concatenate(ps, axis=0)

    def values(blk, p):
        rows = slice(blk * BLOCK, (blk + 1) * BLOCK)
        o = jnp.dot(p, vall[blk * BLOCK:(blk + 3) * BLOCK], preferred_element_type=_F32)
        for g in range(A_GROUP):
            o_ref[rows, g * HEAD_DIM:(g + 1) * HEAD_DIM] = o[g * BLOCK:(g + 1) * BLOCK].astype(o_ref.dtype)

    s_cur = scores(0)
    p_prev = None
    for blk in range(_WIN_QBLOCKS):
        s_next = scores(blk + 1) if blk + 1 < _WIN_QBLOCKS else None
        p_cur = softmax(blk, s_cur)
        if p_prev is not None:
            values(blk - 1, p_prev)
        s_cur, p_prev = s_next, p_cur
    values(_WIN_QBLOCKS - 1, p_prev)


def _winattn(proj, params_a, *, batch, seq):
    t = batch * seq
    nb = seq // BLOCK
    qb = _WIN_QBLOCKS
    assert nb % qb == 0
    nq = nb // qb
    qw = A_GROUP * HEAD_DIM

    def edge_spec(col0, first):
        def imap(b, i, kv):
            n = jnp.clip(i * qb - 1, 0, nb - 1) if first else jnp.clip((i + 1) * qb, 0, nb - 1)
            return (b * nb + n, col0 + kv)
        return pl.BlockSpec((BLOCK, HEAD_DIM), imap)

    def mid_spec(col0):
        return pl.BlockSpec((qb * BLOCK, HEAD_DIM), lambda b, i, kv: (b * nq + i, col0 + kv))

    return pl.pallas_call(
        functools.partial(_winattn_body, seq=seq),
        grid=(batch, nq, A_KV_HEADS),
        in_specs=[
            pl.BlockSpec(memory_space=pltpu.SMEM),
            pl.BlockSpec((qb * BLOCK, qw), lambda b, i, kv: (b * nq + i, kv)),
            edge_spec(_AK_BLK, True), mid_spec(_AK_BLK), edge_spec(_AK_BLK, False),
            edge_spec(_AV_BLK, True), mid_spec(_AV_BLK), edge_spec(_AV_BLK, False),
        ],
        out_specs=pl.BlockSpec((qb * BLOCK, qw), lambda b, i, kv: (b * nq + i, kv)),
        out_shape=jax.ShapeDtypeStruct((t, A_Q_W), _BF16),
        compiler_params=pltpu.CompilerParams(
            dimension_semantics=("parallel", "parallel", "parallel")),
        name="winattn",
    )(params_a, proj, proj, proj, proj, proj, proj, proj)


_ONES_ROWS = 16
_N_SPLIT = 3
_EXP_ZERO = 110.0
_NORM_SLACK = 2.1
_Q_SUBTILES = 4
_FIXED_MAX_SPAN = 60.0


def _split_bf16(t):
    pieces = []
    for _ in range(_N_SPLIT):
        piece = t.astype(_BF16)
        pieces.append(piece)
        t = t - piece.astype(_F32)
    return pieces


def _diffattn_body(par_ref, q_ref, k_ref, v_ref, lq1_ref, lk1_ref, lq2_ref, lk2_ref, g_ref, o_ref,
                   qm_ref, ka_ref, vt_ref, kn_ref, nb_ref, a_ref, p0_ref, p1_ref, m_ref, acc_ref, *, tq, tk, seq):
    h = pl.program_id(1)
    qi = pl.program_id(2)
    nk = seq // tk
    near = tq // tk
    slope = par_ref[h]
    inv_slope = par_ref[B_HEADS + h]
    lane_k = lax.broadcasted_iota(jnp.int32, (tk, HEAD_DIM), 1)

    def half_sq_norms_max(n2):
        return jnp.max(jnp.max(n2, axis=0, keepdims=True), axis=1, keepdims=True)

    def max_sq_norm(x):
        sq = x.astype(_F32) ** 2
        lane = lax.broadcasted_iota(jnp.int32, sq.shape, 1)
        s0 = jnp.sum(jnp.where(lane < B_QK_DIM, sq, 0.0), axis=1, keepdims=True)
        s1 = jnp.sum(jnp.where(lane >= B_QK_DIM, sq, 0.0), axis=1, keepdims=True)
        return half_sq_norms_max(jnp.maximum(s0, s1))

    def max_sq_norm_mxu(x):
        sq = (x.astype(_F32) ** 2).astype(_BF16)
        row = lax.broadcasted_iota(jnp.int32, (HEAD_DIM, HEAD_DIM), 0)
        colm = lax.broadcasted_iota(jnp.int32, (HEAD_DIM, HEAD_DIM), 1)
        half_sum = jnp.where((row // B_QK_DIM) == colm, 1.0, 0.0).astype(_BF16)
        return half_sq_norms_max(jnp.dot(sq, half_sum, preferred_element_type=_F32))

    @pl.when(qi == 0)
    def _():
        r = lax.broadcasted_iota(jnp.int32, (tk, HEAD_DIM), 0).astype(_F32)
        feats = []
        for mp in range(2):
            base = (1 - mp) * B_QK_DIM
            f = jnp.zeros((tk, HEAD_DIM), _F32)
            for c, piece in enumerate(_split_bf16(slope * r)):
                f = jnp.where(lane_k == base + c, piece.astype(_F32), f)
            feats.append(f)
        ones = jnp.ones((_ONES_ROWS, tk), _BF16)
        kn2 = jnp.zeros((1, 1), _F32)
        for t in range(nk):
            vt_ref[t] = jnp.concatenate([v_ref[t * tk:(t + 1) * tk, :].T, ones], axis=0)
            k = k_ref[t * tk:(t + 1) * tk, :]
            kn2 = jnp.maximum(kn2, max_sq_norm_mxu(k))
            ka_ref[0, t * tk:(t + 1) * tk, :] = jnp.where(lane_k < B_QK_DIM, k.astype(_F32), feats[0]).astype(_BF16)
            ka_ref[1, t * tk:(t + 1) * tk, :] = jnp.where(lane_k >= B_QK_DIM, k.astype(_F32), feats[1]).astype(_BF16)
        kn_ref[...] = kn2
        krow = lax.broadcasted_iota(jnp.int32, (tk, tq), 0)
        qcol = lax.broadcasted_iota(jnp.int32, (tk, tq), 1)
        for w in range(near):
            nb_ref[w] = slope * jnp.abs(qcol - krow - w * tk).astype(_F32)

    def query_tile(sub, carry):
        rows = pl.ds(pl.multiple_of(sub * tq, tq), tq)
        q = q_ref[rows, :]
        lane = lax.broadcasted_iota(jnp.int32, (tq, HEAD_DIM), 1)
        first_half = lane < B_QK_DIM
        for mp in range(2):
            own = first_half if mp == 0 else jnp.logical_not(first_half)
            base = (1 - mp) * B_QK_DIM
            sel = jnp.where((lane >= base) & (lane < base + _N_SPLIT), 1.0, 0.0)
            for kind, cols in enumerate((sel, -sel, jnp.zeros_like(sel))):
                qm_ref[kind, mp] = jnp.where(own, q.astype(_F32), cols).astype(_BF16)
        acc_ref[...] = jnp.zeros_like(acc_ref)

        col = lax.broadcasted_iota(jnp.int32, (1, 2 * tq), 1)
        q0 = (qi * _Q_SUBTILES + sub) * tq
        qpos = (q0 + jnp.where(col < tq, col, col - tq)).astype(_F32)

        kd = q0 // tk
        span = _NORM_SLACK * jnp.sqrt(max_sq_norm(q) * kn_ref[...])
        reach = (_EXP_ZERO + span) * inv_slope
        reach = jnp.where(reach < seq, reach, float(seq)).astype(jnp.int32)[0, 0] + 1
        lo = jnp.clip((q0 - reach) // tk, 0, kd)
        hi = jnp.clip((q0 + tq - 1 + reach) // tk, kd + near - 1, nk - 1)
        odd = (hi - lo + 1) % 2 == 1
        grow_hi = odd & (hi < nk - 1)
        hi = hi + grow_hi.astype(jnp.int32)
        lo = lo - (odd & jnp.logical_not(grow_hi)).astype(jnp.int32)
        n_tiles = hi - lo + 1
        n_far = n_tiles - near

        def far_index(t):
            ki = lo + t
            ki = ki + near * (ki >= kd).astype(jnp.int32)
            return ki, (ki > kd).astype(jnp.int32), pl.multiple_of(ki * tk, tk)

        def far_bias(t):
            _, after, k0 = far_index(t)
            sign = (1 - 2 * after).astype(_F32)
            return (sign * slope) * (k0.astype(_F32) - qpos)

        def tile_scores(k0, kind, mp):
            return lax.dot_general(ka_ref[mp, pl.ds(k0, tk), :], qm_ref[kind, mp], _NT, preferred_element_type=_F32)

        fixed_reference = jnp.where(span <= _FIXED_MAX_SPAN, 1, 0)[0, 0] == 1

        @pl.when(fixed_reference)
        def _():
            ref = 0.5 * span

            def probs(k0, kind, shift, p_ref):
                for mp in range(2):
                    cols = slice(mp * tq, (mp + 1) * tq)
                    sh = shift if shift.shape[0] == tk else shift[:, cols]
                    p_ref[:, cols] = jnp.exp(tile_scores(k0, kind, mp) - sh).astype(_BF16)

            def probs_at(pos, p_ref):
                if isinstance(pos, int) and pos < near:
                    probs(pl.multiple_of((kd + pos) * tk, tk), 2, nb_ref[pos] + ref, p_ref)
                else:
                    _, after, k0 = far_index(pos - near)
                    probs(k0, after, ref - far_bias(pos - near), p_ref)

            def values_at(pos, p_ref):
                ki = kd + pos if isinstance(pos, int) and pos < near else far_index(pos - near)[0]
                acc_ref[...] += jnp.dot(vt_ref[ki], p_ref[...], preferred_element_type=_F32)

            def pair(j):
                probs_at(2 * j + 1, p1_ref)
                values_at(2 * j, p0_ref)
                probs_at(2 * j + 2, p0_ref)
                values_at(2 * j + 1, p1_ref)

            probs_at(0, p0_ref)
            n_static = near // 2
            for j in range(n_static):
                pair(j)

            def grouped(count, first):
                def body(i, carry):
                    for u in range(count):
                        pair(first + count * i + u)
                    return carry
                return body

            n_pairs = n_tiles // 2 - 1
            done = jnp.int32(n_static)
            for count in (8, 4, 2, 1):
                trips = (n_pairs - done) // count
                lax.fori_loop(0, trips, grouped(count, done), 0)
                done = done + trips * count
            probs_at(n_tiles - 1, p1_ref)
            values_at(n_tiles - 2, p0_ref)
            values_at(n_tiles - 1, p1_ref)

        @pl.when(jnp.logical_not(fixed_reference))
        def _():
            m_ref[...] = jnp.full_like(m_ref, -jnp.inf)

            def fold(k0, kind, shift, c, ki):
                for mp in range(2):
                    cols = slice(mp * tq, (mp + 1) * tq)
                    a = tile_scores(k0, kind, mp)
                    a_ref[:, cols] = a if shift is None else a - shift
                a = a_ref[...]
                m_old = m_ref[...]
                m_new = jnp.maximum(m_old, jnp.max(a, axis=0, keepdims=True) + c)
                alpha = jnp.exp(m_old - m_new)
                p = jnp.exp(a - (m_new - c)).astype(_BF16)
                acc_ref[...] = alpha * acc_ref[...] + jnp.dot(vt_ref[ki], p, preferred_element_type=_F32)
                m_ref[...] = m_new

            for w in range(near):
                fold(pl.multiple_of((kd + w) * tk, tk), 2, nb_ref[w], jnp.zeros((1, 2 * tq), _F32), kd + w)

            def far(t, carry):
                ki, after, k0 = far_index(t)
                fold(k0, after, None, far_bias(t), ki)
                return carry

            lax.fori_loop(0, n_far, far, 0)

        lam = (jnp.exp(jnp.sum(lq1_ref[...] * lk1_ref[...], axis=-1, keepdims=True))
               - jnp.exp(jnp.sum(lq2_ref[...] * lk2_ref[...], axis=-1, keepdims=True))
               + LAMBDA_INIT)
        acc = acc_ref[...]
        on = acc[0:B_V_DIM] / acc[B_V_DIM:B_V_DIM + 1]
        o = on[:, 0:tq] - lam * on[:, tq:2 * tq]
        y = o * lax.rsqrt(jnp.mean(o * o, axis=0, keepdims=True) + EPS)
        o_ref[rows, :] = ((y.T * g_ref[...]) * (1.0 - LAMBDA_INIT)).astype(o_ref.dtype)
        return carry

    lax.fori_loop(0, _Q_SUBTILES, query_tile, 0)


def _diffattn(proj, params_b, lq1, lk1, lq2, lk2, subln_g, *, batch, seq, tq=1024, tk=256):
    t = batch * seq
    tqs = tq * _Q_SUBTILES
    nq = seq // tqs
    nk = seq // tk
    near = tq // tk
    assert tq % tk == 0 and seq % tqs == 0 and nk % 2 == 0 and nk >= near + 2
    small = lambda shape: pl.BlockSpec(shape, lambda b, h, i: (0, 0))
    return pl.pallas_call(
        functools.partial(_diffattn_body, tq=tq, tk=tk, seq=seq),
        grid=(batch, B_HEADS, nq),
        in_specs=[
            pl.BlockSpec(memory_space=pltpu.SMEM),
            pl.BlockSpec((tqs, HEAD_DIM), lambda b, h, i: (b * nq + i, _BQ_BLK + h)),
            pl.BlockSpec((seq, HEAD_DIM), lambda b, h, i: (b, _BK_BLK + h)),
            pl.BlockSpec((seq, HEAD_DIM), lambda b, h, i: (b, _BV_BLK + h)),
            small((1, B_QK_DIM)), small((1, B_QK_DIM)), small((1, B_QK_DIM)), small((1, B_QK_DIM)),
            small((1, B_V_DIM)),
        ],
        out_specs=pl.BlockSpec((tqs, B_V_DIM), lambda b, h, i: (b * nq + i, h)),
        out_shape=jax.ShapeDtypeStruct((t, B_V_W), _BF16),
        scratch_shapes=[
            pltpu.VMEM((3, 2, tq, HEAD_DIM), _BF16),
            pltpu.VMEM((2, seq, HEAD_DIM), _BF16),
            pltpu.VMEM((nk, B_V_DIM + _ONES_ROWS, tk), _BF16),
            pltpu.VMEM((1, 1), _F32),
            pltpu.VMEM((near, tk, tq), _F32),
            pltpu.VMEM((tk, 2 * tq), _F32),
            pltpu.VMEM((tk, 2 * tq), _BF16),
            pltpu.VMEM((tk, 2 * tq), _BF16),
            pltpu.VMEM((1, 2 * tq), _F32),
            pltpu.VMEM((B_V_DIM + _ONES_ROWS, 2 * tq), _F32),
        ],
        compiler_params=pltpu.CompilerParams(
            dimension_semantics=("parallel", "parallel", "arbitrary"),
            vmem_limit_bytes=_VMEM_LIMIT_BYTES),
        name="diffattn",
    )(params_b, proj, proj, proj, lq1, lk1, lq2, lk2, subln_g)


def _mlp_body(oa_ref, ob_ref, wo_ref, x_ref, gm_ref, wu_ref, wd_ref, gf_ref, o_ref, h_ref, acc_ref):
    j = pl.program_id(1)

    @pl.when(j == 0)
    def _():
        a = jnp.concatenate([oa_ref[...], ob_ref[...]], axis=1)
        x1 = x_ref[...] + jnp.dot(a, wo_ref[...], preferred_element_type=_F32)
        h_ref[...] = (_rms_scale(x1) * gm_ref[...]).astype(_BF16)
        acc_ref[...] = x1

    u = jnp.maximum(jnp.dot(h_ref[...], wu_ref[...], preferred_element_type=_F32), 0.0)
    acc_ref[...] += jnp.dot((u * u).astype(_BF16), wd_ref[...], preferred_element_type=_F32)

    @pl.when(j == pl.num_programs(1) - 1)
    def _():
        o_ref[...] = _rms_scale(acc_ref[...]) * gf_ref[...]


def _mlp(oa, ob, w_out, x2d, g_mlp, w_up, w_down, g_final, *, tm=512, tf=1024):
    t = x2d.shape[0]
    return pl.pallas_call(
        _mlp_body,
        grid=(t // tm, D_FF // tf),
        in_specs=[
            pl.BlockSpec((tm, A_Q_W), lambda i, j: (i, 0)),
            pl.BlockSpec((tm, B_V_W), lambda i, j: (i, 0)),
            pl.BlockSpec((MIX_WIDTH, D_MODEL), lambda i, j: (0, 0), pipeline_mode=pl.Buffered(1)),
            pl.BlockSpec((tm, D_MODEL), lambda i, j: (i, 0)),
            pl.BlockSpec((1, D_MODEL), lambda i, j: (0, 0)),
            pl.BlockSpec((D_MODEL, tf), lambda i, j: (0, j)),
            pl.BlockSpec((tf, D_MODEL), lambda i, j: (j, 0)),
            pl.BlockSpec((1, D_MODEL), lambda i, j: (0, 0)),
        ],
        out_specs=pl.BlockSpec((tm, D_MODEL), lambda i, j: (i, 0)),
        out_shape=jax.ShapeDtypeStruct((t, D_MODEL), _F32),
        scratch_shapes=[pltpu.VMEM((tm, D_MODEL), _BF16), pltpu.VMEM((tm, D_MODEL), _F32)],
        compiler_params=pltpu.CompilerParams(
            dimension_semantics=("parallel", "arbitrary"),
            vmem_limit_bytes=_VMEM_LIMIT_BYTES),
        name="mlp",
    )(oa, ob, w_out, x2d, g_mlp, w_up, w_down, g_final)


def _alibi_slopes():
    i = jnp.arange(1, N_HEADS_TOTAL + 1, dtype=_F32)
    s = jnp.exp2(-8.0 / N_HEADS_TOTAL * i)
    return s[0::2], s[1::2]


def _encoder(x, p):
    batch, seq = x.shape[0], x.shape[1]
    x2d = x.reshape(batch * seq, D_MODEL)
    proj = _inproj(x2d, p["g_attn"], p["w_in"], p["colscale"])
    oa = _winattn(proj, p["params_a"], batch=batch, seq=seq)
    ob = _diffattn(proj, p["params_b"], p["lq1"], p["lk1"], p["lq2"], p["lk2"], p["subln_g"],
                   batch=batch, seq=seq)
    y = _mlp(oa, ob, p["w_out"], x2d, p["g_mlp"], p["w_up"], p["w_down"], p["g_final"])
    return y.reshape(batch, seq, D_MODEL)


def kernel(x_prompt, x_sample, norm_attn_g, w_in, sink_logits, lambda_q1, lambda_k1, lambda_q2, lambda_k2,
           diff_subln_g, w_out, norm_mlp_g, w_up, w_down, norm_final_g):
    slopes_a, slopes_b = _alibi_slopes()
    col = jnp.arange(IN_WIDTH)
    in_bq = (col >= _BQ_BLK * HEAD_DIM) & (col < _BK_BLK * HEAD_DIM)
    p = {
        "g_attn": norm_attn_g[0].reshape(1, D_MODEL).astype(_F32),
        "w_in": w_in[0].astype(_BF16),
        "colscale": jnp.where(in_bq, 1.0 / math.sqrt(B_QK_DIM), 1.0).astype(_F32).reshape(1, IN_WIDTH),
        "params_a": jnp.concatenate([slopes_a, sink_logits[0].astype(_F32)]),
        "params_b": jnp.concatenate([slopes_b, 1.0 / slopes_b]),
        "lq1": lambda_q1[0].reshape(1, B_QK_DIM).astype(_F32),
        "lk1": lambda_k1[0].reshape(1, B_QK_DIM).astype(_F32),
        "lq2": lambda_q2[0].reshape(1, B_QK_DIM).astype(_F32),
        "lk2": lambda_k2[0].reshape(1, B_QK_DIM).astype(_F32),
        "subln_g": diff_subln_g[0].reshape(1, B_V_DIM).astype(_F32),
        "w_out": w_out[0].astype(_BF16),
        "g_mlp": norm_mlp_g[0].reshape(1, D_MODEL).astype(_F32),
        "w_up": w_up[0].astype(_BF16),
        "w_down": w_down[0].astype(_BF16),
        "g_final": norm_final_g.reshape(1, D_MODEL).astype(_F32),
    }
    return (_encoder(x_prompt, p), _encoder(x_sample, p))
```
